```python
import jax
import jax.numpy as jnp
from jax import lax
import numpy as np

D_MODEL = 1024
BATCH = 2
SEQ = 16384
DEPTH = 4

GRID_W = 64
CTX_LEN = 256
HEAD_DIM = 64
ROPE_THETA = 10000.0
EPS = 1e-6
NEG = -1e30
N_MOD = 6

A_HEADS = 8
A_KV_HEADS = 2
A_BLOCK = 128
A_WINDOW = 128
B_HEADS = 8
NA_ROWS = 8
NA_COLS = 16
C_HEADS = 4
C_DK = 128
C_DV = 256
C_GATE_RANK = 16
C_GATE_TAU = 16.0
C_CHUNK = 64
PEER_HEADS = 8
PEER_NKEYS = 128
PEER_EXPERTS = PEER_NKEYS * PEER_NKEYS
PEER_QDIM = 256
PEER_HALF = PEER_QDIM // 2
PEER_TOPK = 16
PEER_BLOCK = 128

A_Q = A_HEADS * HEAD_DIM
A_KV = A_KV_HEADS * HEAD_DIM
B_W = B_HEADS * HEAD_DIM
C_K = C_HEADS * C_DK
C_V = C_HEADS * C_DV
KV_SPLITS = (A_KV, A_KV, B_W, B_W, C_K, C_V, 2 * C_GATE_RANK)
Q_SPLITS = (A_Q, B_W, C_K, C_V, 3 * D_MODEL)
N_KV_COLS = sum(KV_SPLITS)
IN_WIDTH = N_KV_COLS + sum(Q_SPLITS)

kernel_name = 'hybrid_dit_parallel_mixers_peer'


def _rmsnorm(x, g):
    xf = x.astype(jnp.float32)
    y = xf * lax.rsqrt(jnp.mean(xf * xf, axis=-1, keepdims=True) + EPS)
    return (y * g.astype(jnp.float32)).astype(x.dtype)


def _modulate(x, g, shift, scale):
    return _rmsnorm(x, g) * (1.0 + scale) + shift


def _split(t, sizes):
    return jnp.split(t, np.cumsum(sizes)[:-1].tolist(), axis=-1)


def _heads(t, n):
    return t.reshape(t.shape[:-1] + (n, t.shape[-1] // n))


def _axial_rope(S):
    t = jnp.arange(S)
    row = (t // GRID_W).astype(jnp.float32)[:, None]
    col = (t % GRID_W).astype(jnp.float32)[:, None]
    nf = HEAD_DIM // 4
    inv = ROPE_THETA ** (-jnp.arange(nf, dtype=jnp.float32) / nf)
    ang = jnp.concatenate([row * inv, row * inv, col * inv, col * inv], axis=-1)
    return jnp.cos(ang), jnp.sin(ang)


def _rope(x, cos, sin):
    x1, x2, x3, x4 = jnp.split(x, 4, axis=-1)
    rot = jnp.concatenate([-x2, x1, -x4, x3], axis=-1)
    return (x * cos[:, None] + rot * sin[:, None]).astype(x.dtype)


def _dense_attention(q, k, v, sink):
    B, L, H, dh = q.shape
    G = k.shape[2]
    R = H // G
    s = jnp.einsum('bqgrd,bkgd->bgrqk', q.reshape(B, L, G, R, dh), k).astype(jnp.float32) * dh ** -0.5
    if sink is not None:
        s_sink = jnp.broadcast_to(sink.reshape(G, R, 1, 1).astype(jnp.float32), s.shape[:-1] + (1,))
        s = jnp.concatenate([s, s_sink], axis=-1)
    p = jax.nn.softmax(s, axis=-1)[..., :L].astype(v.dtype)
    return jnp.einsum('bgrqk,bkgd->bqgrd', p, v).reshape(B, L, H * dh)


def _window_attention(q, k, v, kc, vc, sink):
    B, S, H, dh = q.shape
    G = k.shape[2]
    R = H // G
    L = kc.shape[1]
    nb = S // A_BLOCK
    qb = q.reshape(B, nb, A_BLOCK, G, R, dh)
    pad = ((0, 0), (A_BLOCK, A_BLOCK), (0, 0), (0, 0))

    def band(t):
        tb = jnp.pad(t, pad).reshape(B, nb + 2, A_BLOCK, G, dh)
        return jnp.concatenate([tb[:, :-2], tb[:, 1:-1], tb[:, 2:]], axis=2)

    kw, vw = band(k), band(v)
    qi = jnp.arange(nb)[:, None, None] * A_BLOCK + jnp.arange(A_BLOCK)[None, :, None]
    kj = (jnp.arange(nb)[:, None, None] - 1) * A_BLOCK + jnp.arange(3 * A_BLOCK)[None, None, :]
    valid = (jnp.abs(kj - qi) <= A_WINDOW) & (kj >= 0) & (kj < S)
    scale = dh ** -0.5
    s_loc = jnp.einsum('bnqgrd,bnkgd->bngrqk', qb, kw).astype(jnp.float32) * scale
    s_loc = jnp.where(valid[None, :, None, None], s_loc, NEG)
    s_ctx = jnp.einsum('bnqgrd,bcgd->bngrqc', qb, kc).astype(jnp.float32) * scale
    s_sink = jnp.broadcast_to(sink.reshape(G, R, 1, 1).astype(jnp.float32), s_loc.shape[:-1] + (1,))
    p = jax.nn.softmax(jnp.concatenate([s_loc, s_ctx, s_sink], axis=-1), axis=-1).astype(v.dtype)
    nk = 3 * A_BLOCK
    o = (jnp.einsum('bngrqk,bnkgd->bnqgrd', p[..., :nk], vw)
         + jnp.einsum('bngrqc,bcgd->bnqgrd', p[..., nk:nk + L], vc))
    return o.reshape(B, S, H * dh)


def _neighbourhood_attention(q, k, v, kc, vc, rpb):
    B, S, H, dh = q.shape
    rows = S // GRID_W
    kh = min(NA_ROWS, rows)
    r = jnp.arange(rows)
    row_idx = jnp.clip(r - kh // 2, 0, rows - kh)[:, None] + jnp.arange(kh)[None, :]
    c = jnp.arange(GRID_W)
    col_start = jnp.clip(c - NA_COLS // 2, 0, GRID_W - NA_COLS)
    col_valid = (c[None, :] >= col_start[:, None]) & (c[None, :] < col_start[:, None] + NA_COLS)
    qg = q.reshape(B, rows, GRID_W, H, dh)
    kg = k.reshape(B, rows, GRID_W, H, dh)[:, row_idx]
    vg = v.reshape(B, rows, GRID_W, H, dh)[:, row_idx]
    scale = dh ** -0.5
    s = jnp.einsum('brqhd,brjkhd->bhrqjk', qg, kg).astype(jnp.float32) * scale
    dr = row_idx - r[:, None] + (NA_ROWS - 1)
    dc = jnp.clip(c[None, :] - c[:, None], -(NA_COLS - 1), NA_COLS - 1) + (NA_COLS - 1)
    bias = rpb[:, dr[:, None, :, None], dc[None, :, None, :]]
    s = jnp.where(col_valid[:, None, :], s + bias[None].astype(jnp.float32), NEG)
    s = s.reshape(B, H, rows, GRID_W, kh * GRID_W)
    s_ctx = jnp.einsum('brqhd,bchd->bhrqc', qg, kc).astype(jnp.float32) * scale
    p = jax.nn.softmax(jnp.concatenate([s, s_ctx], axis=-1), axis=-1).astype(v.dtype)
    nk = kh * GRID_W
    p_loc = p[..., :nk].reshape(B, H, rows, GRID_W, kh, GRID_W)
    o = (jnp.einsum('bhrqjk,brjkhd->brqhd', p_loc, vg)
         + jnp.einsum('bhrqc,bchd->brqhd', p[..., nk:], vc))
    return o.reshape(B, S, H * dh)


def _gla_decay(a, w2, b):
    z = (a @ w2 + b).astype(jnp.float32)
    return _heads(jax.nn.log_sigmoid(z) / C_GATE_TAU, C_HEADS)


def _gla_scan(q, k, v, la, s0):
    q, k, v = q.astype(jnp.float32), k.astype(jnp.float32), v.astype(jnp.float32)
    B, T, H, dk = q.shape
    dv = v.shape[-1]
    n = T // C_CHUNK

    def chunks(t):
        return t.reshape(B, n, C_CHUNK, H, t.shape[-1]).transpose(1, 0, 3, 2, 4)

    causal = jnp.tril(jnp.ones((C_CHUNK, C_CHUNK), bool))[:, :, None]

    def step(s, inp):
        qc, kc, vc, lac = inp
        b = jnp.cumsum(lac, axis=2)
        diff = jnp.where(causal, b[:, :, :, None, :] - b[:, :, None, :, :], 0.0)
        decay = jnp.where(causal, jnp.exp(diff), 0.0)
        att = jnp.einsum('bhid,bhjd,bhijd->bhij', qc, kc, decay)
        o = jnp.einsum('bhij,bhjv->bhiv', att, vc) + jnp.einsum('bhid,bhdv->bhiv', qc * jnp.exp(b), s)
        b_last = b[:, :, -1:]
        s_new = (jnp.exp(b_last[:, :, 0])[..., None] * s
                 + jnp.einsum('bhjd,bhjv->bhdv', kc * jnp.exp(b_last - b), vc))
        return s_new, o

    s_fin, o = lax.scan(step, s0, (chunks(q), chunks(k), chunks(v), chunks(la)))
    return o.transpose(1, 0, 3, 2, 4).reshape(B, T, H, dv), s_fin


def _gla_final_state(k, v, la):
    b = jnp.cumsum(la, axis=1)
    return jnp.einsum('bthd,bthv->bhdv', k.astype(jnp.float32) * jnp.exp(b[:, -1:] - b), v.astype(jnp.float32))


def _gla_out(o, r, g):
    y = _rmsnorm(o, g).astype(r.dtype) * jax.nn.silu(_heads(r, C_HEADS))
    return y.reshape(y.shape[:-2] + (C_V,))


def _flip(t):
    return jnp.flip(t, axis=1)


def _merge(oA, oB, oC, gates, p):
    gA, gB, gC = jnp.split(gates, 3, axis=-1)
    y = (jax.nn.sigmoid(gA) * (oA @ p['w_oa'])
         + jax.nn.sigmoid(gB) * (oB @ p['w_ob'])
         + jax.nn.sigmoid(gC) * (oC @ p['w_oc']))
    return y @ p['w_out']


def _token_mixers(h, hc, p, cos, sin, last):
    B = h.shape[0]
    w_in = p['w_in']
    proj = h @ w_in
    kA, vA, kB, vB, kC, vC, aC = _split(proj[..., :N_KV_COLS], KV_SPLITS)
    qA, qB, qC, rC, gates = _split(proj[..., N_KV_COLS:], Q_SPLITS)
    cproj = hc @ (w_in[:, :N_KV_COLS] if last else w_in)
    ckA, cvA, ckB, cvB, ckC, cvC, caC = _split(cproj[..., :N_KV_COLS], KV_SPLITS)
    if not last:
        cqA, cqB, cqC, crC, cgates = _split(cproj[..., N_KV_COLS:], Q_SPLITS)

    qa = _rope(_rmsnorm(_heads(qA, A_HEADS), p['qn_a']), cos, sin)
    ka = _rope(_rmsnorm(_heads(kA, A_KV_HEADS), p['kn_a']), cos, sin)
    cka = _rmsnorm(_heads(ckA, A_KV_HEADS), p['kn_a'])
    cva = _heads(cvA, A_KV_HEADS)
    oA = _window_attention(qa, ka, _heads(vA, A_KV_HEADS), cka, cva, p['sink_a'])

    qb = _rmsnorm(_heads(qB, B_HEADS), p['qn_b'])
    kb = _rmsnorm(_heads(kB, B_HEADS), p['kn_b'])
    ckb = _rmsnorm(_heads(ckB, B_HEADS), p['kn_b'])
    cvb = _heads(cvB, B_HEADS)
    oB = _neighbourhood_attention(qb, kb, _heads(vB, B_HEADS), ckb, cvb, p['rpb_b'])

    ckc, cvc = _heads(ckC, C_HEADS), _heads(cvC, C_HEADS)
    caf, cab = jnp.split(caC, 2, axis=-1)
    cla_f = _gla_decay(caf, p['wa2_f'], p['ba_f'])
    cla_b = _gla_decay(cab, p['wa2_b'], p['ba_b'])
    if last:
        s_f = _gla_final_state(ckc, cvc, cla_f)
        s_b = _gla_final_state(_flip(ckc), _flip(cvc), _flip(cla_b))
    else:
        cqc = _heads(cqC, C_HEADS) * C_DK ** -0.5
        zeros = jnp.zeros((B, C_HEADS, C_DK, C_DV), jnp.float32)
        co_f, s_f = _gla_scan(cqc, ckc, cvc, cla_f, zeros)
        co_b, s_b = _gla_scan(_flip(cqc), _flip(ckc), _flip(cvc), _flip(cla_b), zeros)
        coC = _gla_out(co_f + _flip(co_b), crC, p['gn_c'])
    qc_ = _heads(qC, C_HEADS) * C_DK ** -0.5
    kc_, vc_ = _heads(kC, C_HEADS), _heads(vC, C_HEADS)
    af, ab = jnp.split(aC, 2, axis=-1)
    la_f = _gla_decay(af, p['wa2_f'], p['ba_f'])
    la_b = _gla_decay(ab, p['wa2_b'], p['ba_b'])
    o_f, _ = _gla_scan(qc_, kc_, vc_, la_f, s_f)
    o_b, _ = _gla_scan(_flip(qc_), _flip(kc_), _flip(vc_), _flip(la_b), s_b)
    oC = _gla_out(o_f + _flip(o_b), rC, p['gn_c'])

    y = _merge(oA, oB, oC, gates, p)
    if last:
        return y, None
    coA = _dense_attention(_rmsnorm(_heads(cqA, A_HEADS), p['qn_a']), cka, cva, p['sink_a'])
    coB = _dense_attention(_rmsnorm(_heads(cqB, B_HEADS), p['qn_b']), ckb, cvb, None)
    yc = _merge(coA, coB, coC, cgates, p)
    return y, yc


def _peer(h, p):
    T, D = h.shape
    u_tab, v_tab = p['peer_u'], p['peer_v']

    def block(xb):
        q = _rmsnorm((xb @ p['w_pq']).reshape(PEER_BLOCK, PEER_HEADS, 2, PEER_HALF), p['pq_g'])
        s = jnp.einsum('tphd,phnd->tphn', q, p['sub_keys']).astype(jnp.float32)
        s1, i1 = lax.top_k(s[:, :, 0], PEER_TOPK)
        s2, i2 = lax.top_k(s[:, :, 1], PEER_TOPK)
        cand = (s1[..., :, None] + s2[..., None, :]).reshape(PEER_BLOCK, PEER_HEADS, PEER_TOPK * PEER_TOPK)
        cidx = (i1[..., :, None] * PEER_NKEYS + i2[..., None, :]).reshape(PEER_BLOCK, PEER_HEADS, PEER_TOPK * PEER_TOPK)
        top, pos = lax.top_k(cand, PEER_TOPK)
        idx = jnp.take_along_axis(cidx, pos, axis=-1)
        g = jax.nn.softmax(top, axis=-1).astype(xb.dtype)
        u = u_tab[idx]
        v = v_tab[idx]
        act = jax.nn.gelu(jnp.einsum('td,tpkd->tpk', xb, u), approximate=False)
        return jnp.einsum('tpk,tpkd->td', g * act, v)

    return lax.map(block, h.reshape(T // PEER_BLOCK, PEER_BLOCK, D)).reshape(T, D)


def _layer(x, ctx, c_act, cctx_act, p, cos, sin, last):
    B, S, D = x.shape
    L = ctx.shape[1]
    sh1, sc1, g1, sh2, sc2, g2 = jnp.split((c_act @ p['w_mod'] + p['b_mod'])[:, None, :], N_MOD, axis=-1)
    csh1, csc1, cg1, csh2, csc2, cg2 = jnp.split(cctx_act @ p['w_mod'] + p['b_mod'], N_MOD, axis=-1)
    h = _modulate(x, p['norm1'], sh1, sc1)
    hc = _modulate(ctx, p['norm1'], csh1, csc1)
    y, yc = _token_mixers(h, hc, p, cos, sin, last)
    x = x + g1 * y
    h2 = _modulate(x, p['norm2'], sh2, sc2)
    if last:
        return x + g2 * _peer(h2.reshape(B * S, D), p).reshape(B, S, D), ctx
    ctx = ctx + cg1 * yc
    hc2 = _modulate(ctx, p['norm2'], csh2, csc2)
    out = _peer(jnp.concatenate([h2.reshape(B * S, D), hc2.reshape(B * L, D)], axis=0), p)
    x = x + g2 * out[:B * S].reshape(B, S, D)
    ctx = ctx + cg2 * out[B * S:].reshape(B, L, D)
    return x, ctx


def setup_inputs(seed: int = 0) -> dict:
    key = jax.random.key(seed)
    ks = iter(jax.random.split(key, 40))

    def nrm(shape, scale):
        return jax.random.normal(next(ks), shape, jnp.float32) * scale

    D, Ly = D_MODEL, DEPTH
    return {
        'x': nrm((BATCH, SEQ, D), 1.0),
        'c': nrm((BATCH, D), 1.0),
        'ctx': nrm((BATCH, CTX_LEN, D), 1.0),
        'c_ctx': nrm((D,), 1.0),
        'w_mod': nrm((Ly, D, N_MOD * D), 0.5 * D ** -0.5),
        'b_mod': nrm((Ly, N_MOD * D), 0.02),
        'norm1': 1.0 + nrm((Ly, D), 0.05),
        'norm2': 1.0 + nrm((Ly, D), 0.05),
        'w_in': nrm((Ly, D, IN_WIDTH), D ** -0.5),
        'qn_a': 1.0 + nrm((Ly, HEAD_DIM), 0.05),
        'kn_a': 1.0 + nrm((Ly, HEAD_DIM), 0.05),
        'sink_a': nrm((Ly, A_HEADS), 0.5),
        'qn_b': 1.0 + nrm((Ly, HEAD_DIM), 0.05),
        'kn_b': 1.0 + nrm((Ly, HEAD_DIM), 0.05),
        'rpb_b': nrm((Ly, B_HEADS, 2 * NA_ROWS - 1, 2 * NA_COLS - 1), 0.1),
        'wa2_f': nrm((Ly, C_GATE_RANK, C_K), C_GATE_RANK ** -0.5),
        'ba_f': nrm((Ly, C_K), 0.1),
        'wa2_b': nrm((Ly, C_GATE_RANK, C_K), C_GATE_RANK ** -0.5),
        'ba_b': nrm((Ly, C_K), 0.1),
        'gn_c': 1.0 + nrm((Ly, C_DV), 0.05),
        'w_oa': nrm((Ly, A_Q, D), A_Q ** -0.5),
        'w_ob': nrm((Ly, B_W, D), B_W ** -0.5),
        'w_oc': nrm((Ly, C_V, D), C_V ** -0.5),
        'w_out': nrm((Ly, D, D), D ** -0.5),
        'w_pq': nrm((Ly, D, PEER_HEADS * PEER_QDIM), D ** -0.5),
        'pq_g': 1.0 + nrm((Ly, PEER_HALF), 0.05),
        'sub_keys': nrm((Ly, PEER_HEADS, 2, PEER_NKEYS, PEER_HALF), PEER_HALF ** -0.5),
        'peer_u': nrm((Ly, PEER_EXPERTS, D), D ** -0.5),
        'peer_v': nrm((Ly, PEER_EXPERTS, D), 0.5),
    }


def reference(x, c, ctx, c_ctx, w_mod, b_mod, norm1, norm2, w_in, qn_a, kn_a, sink_a, qn_b, kn_b, rpb_b,
              wa2_f, ba_f, wa2_b, ba_b, gn_c, w_oa, w_ob, w_oc, w_out, w_pq, pq_g, sub_keys, peer_u, peer_v):
    c_act = jax.nn.silu(c)
    cctx_act = jax.nn.silu(c_ctx)
    cos, sin = _axial_rope(x.shape[1])
    for i in range(DEPTH):
        p = {
            'w_mod': w_mod[i], 'b_mod': b_mod[i], 'norm1': norm1[i], 'norm2': norm2[i], 'w_in': w_in[i],
            'qn_a': qn_a[i], 'kn_a': kn_a[i], 'sink_a': sink_a[i], 'qn_b': qn_b[i], 'kn_b': kn_b[i],
            'rpb_b': rpb_b[i], 'wa2_f': wa2_f[i], 'ba_f': ba_f[i], 'wa2_b': wa2_b[i], 'ba_b': ba_b[i],
            'gn_c': gn_c[i], 'w_oa': w_oa[i], 'w_ob': w_ob[i], 'w_oc': w_oc[i], 'w_out': w_out[i],
            'w_pq': w_pq[i], 'pq_g': pq_g[i], 'sub_keys': sub_keys[i], 'peer_u': peer_u[i], 'peer_v': peer_v[i],
        }
        x, ctx = _layer(x, ctx, c_act, cctx_act, p, cos, sin, i == DEPTH - 1)
    return x
```

```python
import functools

import numpy as np
import jax
import jax.numpy as jnp
from jax import lax
from jax.experimental import pallas as pl
from jax.experimental.pallas import tpu as pltpu

F32 = jnp.float32
BF16 = jnp.bfloat16

D_MODEL = 1024
DEPTH = 4
GRID_W = 64
HEAD_DIM = 64
ROPE_THETA = 10000.0
EPS = 1e-6
NEG = -1e30
N_MOD = 6
A_HEADS = 8
A_KV_HEADS = 2
A_BLOCK = 128
B_HEADS = 8
NA_ROWS = 8
NA_COLS = 16
C_HEADS = 4
C_DK = 128
C_DV = 256
C_GATE_RANK = 16
C_GATE_TAU = 16.0
C_CHUNK = 64
PEER_HEADS = 8
PEER_NKEYS = 128
PEER_EXPERTS = PEER_NKEYS * PEER_NKEYS
PEER_HALF = 128
PEER_TOPK = 16
PEER_PICKS = PEER_HEADS * PEER_TOPK

LANES = 128
MIB = 1024 * 1024

_SEGS = (
    ("vC", 1792, 1024, 0), ("rC", 4384, 1024, 1024), ("gA", 5408, 1024, 2048),
    ("gB", 6432, 1024, 3072), ("gC", 7456, 1024, 4096), ("kB", 256, 512, 5120),
    ("vB", 768, 512, 5632), ("kC", 1280, 512, 6144), ("qA", 2848, 512, 6656),
    ("qB", 3360, 512, 7168), ("qC", 3872, 512, 7680), ("kA", 0, 128, 8192),
    ("vA", 128, 128, 8320), ("aC", 2816, 32, 8448),
)
PROJ_W = 8704
PROJ_TN = 2176
COLBLK = {name: new // (LANES if w < LANES else w) for name, _, w, new in _SEGS}
PROJ_DT = F32

NA_TR = 4
GATHER_TM = 128
GATHER_NBUF = 8
UV_CHUNKS = 16


def _cparams(sem, vmem_mib):
    return pltpu.CompilerParams(dimension_semantics=sem, vmem_limit_bytes=vmem_mib * MIB)


def _split_bf16(a):
    hi = a.astype(BF16)
    lo = (a - hi.astype(F32)).astype(BF16)
    return hi, lo


def _dot(a, b):
    return jnp.dot(a, b, preferred_element_type=F32)


def _dot_nt(a, b):
    return lax.dot_general(a, b, (((1,), (1,)), ((), ())), preferred_element_type=F32)


def _dot_tn(a, b):
    return lax.dot_general(a, b, (((0,), (0,)), ((), ())), preferred_element_type=F32)


def _dot_precise(a, b):
    ah, al = _split_bf16(a)
    bh, bl = _split_bf16(b)
    return _dot(ah, bh) + _dot(ah, bl) + _dot(al, bh)


def _mod_kernel(c_ref, w_ref, b_ref, o_ref):
    c = c_ref[...]
    act = c * jax.nn.sigmoid(c)
    o_ref[...] = _dot_precise(act, w_ref[...]) + b_ref[...]


def _modulation(c_all, w_mod, b_mod):
    n = w_mod.shape[1]
    tn = D_MODEL
    return pl.pallas_call(
        _mod_kernel,
        grid=(n // tn,),
        in_specs=[pl.BlockSpec((8, D_MODEL), lambda j: (0, 0)),
                  pl.BlockSpec((D_MODEL, tn), lambda j: (0, j)),
                  pl.BlockSpec((1, tn), lambda j: (0, j))],
        out_specs=pl.BlockSpec((8, tn), lambda j: (0, j)),
        out_shape=jax.ShapeDtypeStruct((8, n), F32),
        compiler_params=_cparams(("arbitrary",), 32),
        name="modulation",
    )(c_all, w_mod, b_mod.reshape(1, n))


def _modulated_norm(x, g, sh, sc):
    ms = jnp.mean(x * x, axis=-1, keepdims=True)
    return (x * lax.rsqrt(ms + EPS) * g) * (1.0 + sc) + sh


def _norm_proj_kernel(x_ref, g_ref, sh_ref, sc_ref, w_ref, o_ref):
    h = _modulated_norm(x_ref[0], g_ref[...], sh_ref[0], sc_ref[0])
    o_ref[0] = _dot(h.astype(BF16), w_ref[...]).astype(o_ref.dtype)


def _norm_proj(x, g, sh, sc, w):
    B, T, D = x.shape
    tm = min(T, 512)
    return pl.pallas_call(
        _norm_proj_kernel,
        grid=(PROJ_W // PROJ_TN, B, T // tm),
        in_specs=[pl.BlockSpec((1, tm, D), lambda j, b, i: (b, i, 0)),
                  pl.BlockSpec((1, D), lambda j, b, i: (0, 0)),
                  pl.BlockSpec((1, 1, D), lambda j, b, i: (b, 0, 0)),
                  pl.BlockSpec((1, 1, D), lambda j, b, i: (b, 0, 0)),
                  pl.BlockSpec((D, PROJ_TN), lambda j, b, i: (0, j))],
        out_specs=pl.BlockSpec((1, tm, PROJ_TN), lambda j, b, i: (b, i, j)),
        out_shape=jax.ShapeDtypeStruct((B, T, PROJ_W), PROJ_DT),
        compiler_params=_cparams(("arbitrary", "arbitrary", "arbitrary"), 48),
        name="norm_proj",
    )(x, g.reshape(1, D), sh, sc, w)


def _head_norm(z, g2, m128):
    hi, lo = _split_bf16(z * z)
    ss = _dot(hi, m128) + _dot(lo, m128)
    return z * lax.rsqrt(ss * (1.0 / HEAD_DIM) + EPS) * g2


def _rope128(z, cos, sin, even16):
    rot = jnp.where(even16, -pltpu.roll(z, LANES - 16, 1), pltpu.roll(z, 16, 1))
    return z * cos + rot * sin


def _prep_kernel(qa_ref, qb_ref, kb_ref, ka_ref, cos_ref, sin_ref, gqa_ref, gka_ref, gqb_ref,
                 gkb_ref, m_ref, oqa_ref, oka_ref, oqb_ref, okb_ref):
    m128 = m_ref[...]
    cos = cos_ref[...]
    sin = sin_ref[...]
    lane = lax.broadcasted_iota(jnp.int32, cos.shape, 1)
    even16 = ((lane // 16) % 2) == 0
    scale = HEAD_DIM ** -0.5
    for s in range(4):
        sl = slice(s * LANES, (s + 1) * LANES)
        za = _head_norm(qa_ref[0][:, sl].astype(F32), gqa_ref[...], m128)
        oqa_ref[0, :, sl] = (_rope128(za, cos, sin, even16) * scale).astype(BF16)
        zb = _head_norm(qb_ref[0][:, sl].astype(F32), gqb_ref[...], m128)
        oqb_ref[0, :, sl] = (zb * scale).astype(BF16)
        zk = _head_norm(kb_ref[0][:, sl].astype(F32), gkb_ref[...], m128)
        okb_ref[0, :, sl] = zk.astype(BF16)
    zk = _head_norm(ka_ref[0].astype(F32), gka_ref[...], m128)
    oka_ref[0] = _rope128(zk, cos, sin, even16).astype(BF16)


def _qk_prep(proj, cos, sin, p):
    B, T, _ = proj.shape
    tm = min(T, 256)

    def col(name, w):
        blk = COLBLK[name]
        return pl.BlockSpec((1, tm, w), lambda b, i: (b, i, blk))

    vec = pl.BlockSpec((1, LANES), lambda b, i: (0, 0))
    tab = pl.BlockSpec((tm, LANES), lambda b, i: (i, 0))
    out = lambda w: pl.BlockSpec((1, tm, w), lambda b, i: (b, i, 0))
    return pl.pallas_call(
        _prep_kernel,
        grid=(B, T // tm),
        in_specs=[col("qA", 512), col("qB", 512), col("kB", 512), col("kA", 128), tab, tab,
                  vec, vec, vec, vec, pl.BlockSpec((LANES, LANES), lambda b, i: (0, 0))],
        out_specs=[out(512), out(128), out(512), out(512)],
        out_shape=[jax.ShapeDtypeStruct((B, T, 512), BF16), jax.ShapeDtypeStruct((B, T, 128), BF16),
                   jax.ShapeDtypeStruct((B, T, 512), BF16), jax.ShapeDtypeStruct((B, T, 512), BF16)],
        compiler_params=_cparams(("arbitrary", "arbitrary"), 32),
        name="qk_prep",
    )(proj, proj, proj, proj, cos, sin, p["qn_a2"], p["kn_a2"], p["qn_b2"], p["kn_b2"], p["m128"])


def _softmax_pv(parts, sink):
    m = parts[0][0].max(axis=-1, keepdims=True)
    for s, _ in parts[1:]:
        m = jnp.maximum(m, s.max(axis=-1, keepdims=True))
    if sink is not None:
        m = jnp.maximum(m, sink)
    den = None
    acc = None
    for s, v in parts:
        e = jnp.exp(s - m)
        ds = e.sum(axis=-1, keepdims=True)
        den = ds if den is None else den + ds
        pv = _dot(e.astype(BF16), v)
        acc = pv if acc is None else acc + pv
    if sink is not None:
        den = den + jnp.exp(sink - m)
    return acc / den


def _attn_a_kernel(sink_ref, q_ref, kp_ref, ko_ref, kn_ref, vp_ref, vo_ref, vn_ref, ck_ref, cv_ref, o_ref):
    n = pl.program_id(1)
    nb = pl.num_programs(1)
    q = q_ref[0]
    k_loc = jnp.concatenate([kp_ref[0], ko_ref[0], kn_ref[0]], axis=0)
    v_loc = jnp.concatenate([vp_ref[0], vo_ref[0], vn_ref[0]], axis=0).astype(BF16)
    ck = ck_ref[0]
    cv = cv_ref[0].astype(BF16)
    qi = lax.broadcasted_iota(jnp.int32, (A_BLOCK, 3 * A_BLOCK), 0)
    kc = lax.broadcasted_iota(jnp.int32, (A_BLOCK, 3 * A_BLOCK), 1)
    rel = kc - qi
    valid = (rel >= 0) & (rel <= 2 * A_BLOCK)
    valid = valid & ((n > 0) | (kc >= A_BLOCK)) & ((n < nb - 1) | (kc < 2 * A_BLOCK))
    rep = A_HEADS // A_KV_HEADS
    for h in range(A_HEADS):
        g = h // rep
        qh = q[:, h * HEAD_DIM:(h + 1) * HEAD_DIM]
        gs = slice(g * HEAD_DIM, (g + 1) * HEAD_DIM)
        s_loc = jnp.where(valid, _dot_nt(qh, k_loc[:, gs]), NEG)
        s_ctx = _dot_nt(qh, ck[:, gs])
        o = _softmax_pv([(s_loc, v_loc[:, gs]), (s_ctx, cv[:, gs])], sink_ref[h])
        o_ref[0, :, h * HEAD_DIM:(h + 1) * HEAD_DIM] = o.astype(o_ref.dtype)


def _attn_a(qa, ka, proj, cka, cproj, sink):
    B, S, _ = qa.shape
    L = cka.shape[1]
    nb = S // A_BLOCK
    vblk = COLBLK["vA"]
    blk = lambda f: pl.BlockSpec((1, A_BLOCK, 128), f)
    return pl.pallas_call(
        _attn_a_kernel,
        grid=(B, nb),
        in_specs=[pl.BlockSpec(memory_space=pltpu.SMEM),
                  pl.BlockSpec((1, A_BLOCK, 512), lambda b, n: (b, n, 0)),
                  blk(lambda b, n: (b, jnp.maximum(n - 1, 0), 0)),
                  blk(lambda b, n: (b, n, 0)),
                  blk(lambda b, n: (b, jnp.minimum(n + 1, nb - 1), 0)),
                  blk(lambda b, n: (b, jnp.maximum(n - 1, 0), vblk)),
                  blk(lambda b, n: (b, n, vblk)),
                  blk(lambda b, n: (b, jnp.minimum(n + 1, nb - 1), vblk)),
                  pl.BlockSpec((1, L, 128), lambda b, n: (b, 0, 0)),
                  pl.BlockSpec((1, L, 128), lambda b, n: (b, 0, vblk))],
        out_specs=pl.BlockSpec((1, A_BLOCK, 512), lambda b, n: (b, n, 0)),
        out_shape=jax.ShapeDtypeStruct((B, S, 512), BF16),
        compiler_params=_cparams(("arbitrary", "arbitrary"), 32),
        name="attn_window",
    )(sink, qa, ka, ka, ka, proj, proj, proj, cka, cproj)


def _attn_b_kernel(q_ref, k0_ref, k1_ref, k2_ref, v0_ref, v1_ref, v2_ref, ck_ref, cv_ref, bias_ref, o_ref):
    q = q_ref[0]
    k_loc = jnp.concatenate([k0_ref[0], k1_ref[0], k2_ref[0]], axis=0)
    v_loc = jnp.concatenate([v0_ref[0], v1_ref[0], v2_ref[0]], axis=0).astype(BF16)
    ck = ck_ref[0]
    cv = cv_ref[0].astype(BF16)
    for h in range(B_HEADS):
        hs = slice(h * HEAD_DIM, (h + 1) * HEAD_DIM)
        qh = q[:, hs]
        s_loc = _dot_nt(qh, k_loc[:, hs]) + bias_ref[0, h]
        s_ctx = _dot_nt(qh, ck[:, hs])
        o = _softmax_pv([(s_loc, v_loc[:, hs]), (s_ctx, cv[:, hs])], None)
        o_ref[0, :, hs] = o.astype(o_ref.dtype)


def _na_bias(rpb, rows):
    c = np.arange(GRID_W)
    col_start = np.clip(c - NA_COLS // 2, 0, GRID_W - NA_COLS)
    col_valid = (c[None, :] >= col_start[:, None]) & (c[None, :] < col_start[:, None] + NA_COLS)
    dc = np.clip(c[None, :] - c[:, None], -(NA_COLS - 1), NA_COLS - 1) + (NA_COLS - 1)
    tiles = []
    for r0 in (0, NA_TR, rows - NA_TR):
        rq = r0 + np.arange(NA_TR)
        kr = r0 - NA_TR + np.arange(3 * NA_TR)
        start = np.clip(rq - NA_ROWS // 2, 0, rows - NA_ROWS)
        row_valid = (kr[None, :] >= start[:, None]) & (kr[None, :] < start[:, None] + NA_ROWS)
        dr = np.clip(kr[None, :] - rq[:, None] + (NA_ROWS - 1), 0, 2 * NA_ROWS - 2)
        b = rpb[:, dr[:, None, :, None], dc[None, :, None, :]]
        valid = row_valid[:, None, :, None] & col_valid[None, :, None, :]
        b = jnp.where(jnp.asarray(valid)[None], b, NEG)
        tiles.append(b.reshape(B_HEADS, NA_TR * GRID_W, 3 * NA_TR * GRID_W))
    return jnp.stack(tiles).astype(F32)


def _attn_b(qb, kb, proj, ckb, cproj, bias):
    B, S, _ = qb.shape
    L = ckb.shape[1]
    tq = NA_TR * GRID_W
    ns = S // tq
    vblk = COLBLK["vB"]
    blk = lambda f: pl.BlockSpec((1, tq, 512), f)
    lo = lambda b, i: (b, jnp.maximum(i - 1, 0), 0)
    hi = lambda b, i: (b, jnp.minimum(i + 1, ns - 1), 0)
    variant = lambda b, i: (jnp.where(i == 0, 0, jnp.where(i == ns - 1, 2, 1)), 0, 0, 0)
    return pl.pallas_call(
        _attn_b_kernel,
        grid=(B, ns),
        in_specs=[blk(lambda b, i: (b, i, 0)),
                  blk(lo), blk(lambda b, i: (b, i, 0)), blk(hi),
                  blk(lambda b, i: lo(b, i)[:2] + (vblk,)),
                  blk(lambda b, i: (b, i, vblk)),
                  blk(lambda b, i: hi(b, i)[:2] + (vblk,)),
                  pl.BlockSpec((1, L, 512), lambda b, i: (b, 0, 0)),
                  pl.BlockSpec((1, L, 512), lambda b, i: (b, 0, vblk)),
                  pl.BlockSpec((1, B_HEADS, tq, 3 * tq), variant)],
        out_specs=blk(lambda b, i: (b, i, 0)),
        out_shape=jax.ShapeDtypeStruct((B, S, 512), BF16),
        compiler_params=_cparams(("arbitrary", "arbitrary"), 56),
        name="attn_neighbourhood",
    )(qb, kb, kb, kb, proj, proj, proj, ckb, cproj, bias)


def _ctx_attn_kernel(sink_ref, q_ref, k_ref, v_ref, o_ref, *, kv_heads, use_sink):
    q = q_ref[0]
    k = k_ref[0]
    v = v_ref[0].astype(BF16)
    rep = A_HEADS // kv_heads
    for h in range(A_HEADS):
        g = h // rep
        gs = slice(g * HEAD_DIM, (g + 1) * HEAD_DIM)
        s = _dot_nt(q[:, h * HEAD_DIM:(h + 1) * HEAD_DIM], k[:, gs])
        o = _softmax_pv([(s, v[:, gs])], sink_ref[h] if use_sink else None)
        o_ref[0, :, h * HEAD_DIM:(h + 1) * HEAD_DIM] = o.astype(o_ref.dtype)


def _ctx_attn(q, k, cproj, vname, sink, kv_heads, use_sink):
    B, L, _ = q.shape
    kw = kv_heads * HEAD_DIM
    vblk = COLBLK[vname]
    return pl.pallas_call(
        functools.partial(_ctx_attn_kernel, kv_heads=kv_heads, use_sink=use_sink),
        grid=(B,),
        in_specs=[pl.BlockSpec(memory_space=pltpu.SMEM),
                  pl.BlockSpec((1, L, 512), lambda b: (b, 0, 0)),
                  pl.BlockSpec((1, L, kw), lambda b: (b, 0, 0)),
                  pl.BlockSpec((1, L, kw), lambda b: (b, 0, vblk))],
        out_specs=pl.BlockSpec((1, L, 512), lambda b: (b, 0, 0)),
        out_shape=jax.ShapeDtypeStruct((B, L, 512), BF16),
        compiler_params=_cparams(("arbitrary",), 32),
        name="attn_context",
    )(sink, q, k, cproj)


def _gla_direction(q_ref, k_ref, v_ref, a_ref, w2_ref, b_ref, o_ref, S, forward):
    C = q_ref.shape[1]
    lo = 0 if forward else C_GATE_RANK
    a = a_ref[0][:, lo:lo + C_GATE_RANK].astype(F32)
    z = _dot_precise(a, w2_ref[...]) + b_ref[...]
    la = (jnp.minimum(z, 0.0) - jnp.log1p(jnp.exp(-jnp.abs(z)))) * (1.0 / C_GATE_TAU)
    ri = lax.broadcasted_iota(jnp.int32, (C, C), 0)
    ci = lax.broadcasted_iota(jnp.int32, (C, C), 1)
    keep = (ci <= ri) if forward else (ci >= ri)
    tri = keep.astype(BF16)
    la_hi, la_lo = _split_bf16(la)
    cum = _dot(tri, la_hi) + _dot(tri, la_lo)
    ref_row = C // 2 - 1 if forward else C // 2
    end_row = C - 1 if forward else 0
    scale = C_DK ** -0.5
    for h in range(C_HEADS):
        ks = slice(h * C_DK, (h + 1) * C_DK)
        vs = slice(h * C_DV, (h + 1) * C_DV)
        c = cum[:, ks]
        c_ref = c[ref_row:ref_row + 1]
        c_end = c[end_row:end_row + 1]
        q = q_ref[0][:, ks].astype(F32) * scale
        k = k_ref[0][:, ks].astype(F32)
        v = v_ref[0][:, vs].astype(BF16)
        att = _dot_nt((q * jnp.exp(c - c_ref)).astype(BF16), (k * jnp.exp(c_ref - c)).astype(BF16))
        att = jnp.where(keep, att, 0.0)
        st = S[h]
        o = _dot(att.astype(BF16), v) + _dot_nt((q * jnp.exp(c)).astype(BF16), st.astype(BF16))
        kd = (k * jnp.exp(c_end - c)).astype(BF16)
        S[h] = st * jnp.exp(c_end) + _dot_tn(v, kd)
        o_ref[0, :, vs] = o


def _gla_kernel(qf, kf, vf, af, qb, kb, vb, ab, w2f, w2b, bf, bb, s0f, s0b, of, ob, sff, sfb, Sf, Sb):
    j = pl.program_id(1)

    @pl.when(j == 0)
    def _():
        Sf[...] = s0f[0]
        Sb[...] = s0b[0]

    _gla_direction(qf, kf, vf, af, w2f, bf, of, Sf, True)
    _gla_direction(qb, kb, vb, ab, w2b, bb, ob, Sb, False)

    @pl.when(j == pl.num_programs(1) - 1)
    def _():
        sff[0] = Sf[...]
        sfb[0] = Sb[...]


def _gla(proj, p, s0f, s0b):
    B, T, _ = proj.shape
    C = C_CHUNK
    n = T // C
    fwd = lambda b, j: (b, j)
    bwd = lambda b, j: (b, n - 1 - j)

    def cols(order):
        mk = lambda name, w: pl.BlockSpec((1, C, w), lambda b, j: order(b, j) + (COLBLK[name],))
        return [mk("qC", 512), mk("kC", 512), mk("vC", 1024), mk("aC", 128)]

    wspec = pl.BlockSpec((C_GATE_RANK, 512), lambda b, j: (0, 0))
    bspec = pl.BlockSpec((1, 512), lambda b, j: (0, 0))
    sspec = pl.BlockSpec((1, C_HEADS, C_DV, C_DK), lambda b, j: (b, 0, 0, 0))
    st = jax.ShapeDtypeStruct((B, C_HEADS, C_DV, C_DK), F32)
    return pl.pallas_call(
        _gla_kernel,
        grid=(B, n),
        in_specs=cols(fwd) + cols(bwd) + [wspec, wspec, bspec, bspec, sspec, sspec],
        out_specs=[pl.BlockSpec((1, C, 1024), lambda b, j: (b, j, 0)),
                   pl.BlockSpec((1, C, 1024), lambda b, j: (b, n - 1 - j, 0)),
                   sspec, sspec],
        out_shape=[jax.ShapeDtypeStruct((B, T, 1024), F32), jax.ShapeDtypeStruct((B, T, 1024), F32), st, st],
        scratch_shapes=[pltpu.VMEM((C_HEADS, C_DV, C_DK), F32), pltpu.VMEM((C_HEADS, C_DV, C_DK), F32)],
        compiler_params=_cparams(("arbitrary", "arbitrary"), 32),
        name="gla_scan",
    )(proj, proj, proj, proj, proj, proj, proj, proj, p["wa2_f"], p["wa2_b"], p["ba_f"], p["ba_b"], s0f, s0b)


def _merge_kernel(oa_ref, ob_ref, of_ref, obk_ref, r_ref, ga_ref, gb_ref, gc_ref, x_ref, g1_ref, sh_ref,
                  sc_ref, n2_ref, gn_ref, woa_ref, wob_ref, woc_ref, wout_ref, xo_ref, h2_ref):
    o = of_ref[0] + obk_ref[0]
    r = r_ref[0].astype(F32)
    parts = []
    for h in range(C_HEADS):
        vs = slice(h * C_DV, (h + 1) * C_DV)
        oh = o[:, vs]
        ms = jnp.mean(oh * oh, axis=-1, keepdims=True)
        rh = r[:, vs]
        parts.append((oh * lax.rsqrt(ms + EPS) * gn_ref[...]) * (rh * jax.nn.sigmoid(rh)))
    oc = jnp.concatenate(parts, axis=-1).astype(BF16)
    y = (jax.nn.sigmoid(ga_ref[0].astype(F32)) * _dot(oa_ref[0], woa_ref[...])
         + jax.nn.sigmoid(gb_ref[0].astype(F32)) * _dot(ob_ref[0], wob_ref[...])
         + jax.nn.sigmoid(gc_ref[0].astype(F32)) * _dot(oc, woc_ref[...]))
    xn = x_ref[0] + g1_ref[0] * _dot(y.astype(BF16), wout_ref[...])
    xo_ref[0] = xn
    h2_ref[0] = _modulated_norm(xn, n2_ref[...], sh_ref[0], sc_ref[0])


def _merge(oa, ob, of, obk, proj, x, g1, sh2, sc2, p):
    B, T, D = x.shape
    tm = min(T, 256)
    tok = lambda w: pl.BlockSpec((1, tm, w), lambda b, i: (b, i, 0))
    col = lambda name: pl.BlockSpec((1, tm, 1024), lambda b, i: (b, i, COLBLK[name]))
    mod = pl.BlockSpec((1, 1, D), lambda b, i: (b, 0, 0))
    full = lambda r, c: pl.BlockSpec((r, c), lambda b, i: (0, 0))
    return pl.pallas_call(
        _merge_kernel,
        grid=(B, T // tm),
        in_specs=[tok(512), tok(512), tok(1024), tok(1024), col("rC"), col("gA"), col("gB"), col("gC"),
                  tok(D), mod, mod, mod, full(1, D), full(1, C_DV),
                  full(512, D), full(512, D), full(1024, D), full(D, D)],
        out_specs=[tok(D), tok(D)],
        out_shape=[jax.ShapeDtypeStruct((B, T, D), F32), jax.ShapeDtypeStruct((B, T, D), F32)],
        compiler_params=_cparams(("arbitrary", "arbitrary"), 56),
        name="merge",
    )(oa, ob, of, obk, proj, proj, proj, proj, x, g1, sh2, sc2, p["norm2"], p["gn_c"],
      p["w_oa"], p["w_ob"], p["w_oc"], p["w_out"])


def _topk_rows(arr, rowid, k, payload=None):
    n = arr.shape[0]
    vals, ids = [], []
    for _ in range(k):
        m = arr.max(axis=0, keepdims=True)
        sel = jnp.where(arr == m, rowid, n).min(axis=0, keepdims=True)
        hit = rowid == sel
        vals.append(m)
        ids.append(sel if payload is None else jnp.where(hit, payload, -1).max(axis=0, keepdims=True))
        arr = jnp.where(hit, -jnp.inf, arr)
    return jnp.concatenate(vals, axis=0), jnp.concatenate(ids, axis=0)


def _route_kernel(h_ref, wh_ref, wl_ref, g_ref, kh_ref, kl_ref, idx_ref, gate_ref):
    tm = h_ref.shape[0]
    hh, hl = _split_bf16(h_ref[...])
    q = _dot(hh, wh_ref[...]) + _dot(hh, wl_ref[...]) + _dot(hl, wh_ref[...])
    rowid = lax.broadcasted_iota(jnp.int32, (PEER_NKEYS, tm), 0)
    pos = lax.broadcasted_iota(jnp.int32, (PEER_TOPK * PEER_TOPK, tm), 0)
    for p in range(PEER_HEADS):
        tops = []
        for half in range(2):
            seg = q[:, (2 * p + half) * PEER_HALF:(2 * p + half + 1) * PEER_HALF]
            ms = jnp.mean(seg * seg, axis=-1, keepdims=True)
            qh, ql = _split_bf16(seg * lax.rsqrt(ms + EPS) * g_ref[...])
            kh, kl = kh_ref[p, half], kl_ref[p, half]
            s_t = _dot_nt(kh, qh) + _dot_nt(kh, ql) + _dot_nt(kl, qh)
            tops.append(_topk_rows(s_t, rowid, PEER_TOPK))
        (s1, i1), (s2, i2) = tops
        cand = jnp.concatenate([s1[a:a + 1] + s2 for a in range(PEER_TOPK)], axis=0)
        cidx = jnp.concatenate([i1[a:a + 1] * PEER_NKEYS + i2 for a in range(PEER_TOPK)], axis=0)
        top, eid = _topk_rows(cand, pos, PEER_TOPK, payload=cidx)
        e = jnp.exp(top - top[0:1])
        rs = slice(p * PEER_TOPK, (p + 1) * PEER_TOPK)
        idx_ref[rs, :] = eid
        gate_ref[rs, :] = e / e.sum(axis=0, keepdims=True)


def _peer_route(h2, p):
    T, D = h2.shape
    tm = min(T, 256)
    wspec = pl.BlockSpec((D, 2 * PEER_HEADS * PEER_HALF), lambda i: (0, 0))
    kspec = pl.BlockSpec((PEER_HEADS, 2, PEER_NKEYS, PEER_HALF), lambda i: (0, 0, 0, 0))
    return pl.pallas_call(
        _route_kernel,
        grid=(T // tm,),
        in_specs=[pl.BlockSpec((tm, D), lambda i: (i, 0)), wspec, wspec,
                  pl.BlockSpec((1, PEER_HALF), lambda i: (0, 0)), kspec, kspec],
        out_specs=[pl.BlockSpec((PEER_PICKS, tm), lambda i: (0, i)),
                   pl.BlockSpec((PEER_PICKS, tm), lambda i: (0, i))],
        out_shape=[jax.ShapeDtypeStruct((PEER_PICKS, T), jnp.int32),
                   jax.ShapeDtypeStruct((PEER_PICKS, T), F32)],
        compiler_params=_cparams(("arbitrary",), 48),
        name="peer_route",
    )(h2, p["w_pq_hi"], p["w_pq_lo"], p["pq_g"], p["sub_keys_hi"], p["sub_keys_lo"])


def _gather_kernel(idx_ref, gate_ref, h_ref, x_ref, g2_ref, uv_hbm, o_ref, buf, sem):
    tm = h_ref.shape[0]
    nbuf = buf.shape[0]

    def issue(t, slot):
        for k in range(PEER_PICKS):
            row = pl.multiple_of(idx_ref[t * PEER_PICKS + k] * UV_CHUNKS, UV_CHUNKS)
            pltpu.make_async_copy(uv_hbm.at[pl.ds(row, UV_CHUNKS)],
                                  buf.at[slot, pl.ds(k * UV_CHUNKS, UV_CHUNKS)], sem.at[slot]).start()

    for t0 in range(nbuf - 1):
        issue(t0, t0)

    lane = lax.broadcasted_iota(jnp.int32, (PEER_PICKS, tm), 1)
    half = UV_CHUNKS // 2

    def token(t, slot, h_row):
        pltpu.make_async_copy(uv_hbm.at[pl.ds(0, PEER_PICKS * UV_CHUNKS)], buf.at[slot], sem.at[slot]).wait()
        acc = None
        for c in range(half):
            u_c = buf[slot, pl.ds(c, PEER_PICKS, stride=UV_CHUNKS), :]
            term = u_c * h_row[:, c * LANES:(c + 1) * LANES]
            acc = term if acc is None else acc + term
        d = acc.sum(axis=1, keepdims=True)
        act = 0.5 * d * (1.0 + lax.erf(d * (2.0 ** -0.5)))
        gcol = jnp.where(lane == t, gate_ref[...], 0.0).sum(axis=1, keepdims=True)
        w = gcol * act
        mix = []
        for c in range(half):
            v_c = buf[slot, pl.ds(half + c, PEER_PICKS, stride=UV_CHUNKS), :]
            mix.append((v_c * w).sum(axis=0, keepdims=True))
        return jnp.concatenate(mix, axis=1)

    def group(g, carry):
        t0 = pl.multiple_of(g * nbuf, nbuf)
        h_rows = h_ref[pl.ds(t0, nbuf), :]
        outs = []
        for j in range(nbuf):
            nxt = t0 + j + nbuf - 1

            @pl.when(nxt < tm)
            def _():
                issue(nxt, (j + nbuf - 1) % nbuf)

            outs.append(token(t0 + j, j, h_rows[j:j + 1]))
        o_ref[pl.ds(t0, nbuf), :] = x_ref[pl.ds(t0, nbuf), :] + g2_ref[0] * jnp.concatenate(outs, axis=0)
        return carry

    lax.fori_loop(0, tm // nbuf, group, 0)


def _peer_gather(idx_flat, gate_t, h2, x, g2, uv, tokens_per_batch):
    T, D = h2.shape
    tm = GATHER_TM
    return pl.pallas_call(
        _gather_kernel,
        grid=(T // tm,),
        in_specs=[pl.BlockSpec((tm * PEER_PICKS,), lambda i: (i,), memory_space=pltpu.SMEM),
                  pl.BlockSpec((PEER_PICKS, tm), lambda i: (0, i)),
                  pl.BlockSpec((tm, D), lambda i: (i, 0)),
                  pl.BlockSpec((tm, D), lambda i: (i, 0)),
                  pl.BlockSpec((1, 1, D), lambda i: ((i * tm) // tokens_per_batch, 0, 0)),
                  pl.BlockSpec(memory_space=pl.ANY)],
        out_specs=pl.BlockSpec((tm, D), lambda i: (i, 0)),
        out_shape=jax.ShapeDtypeStruct((T, D), F32),
        scratch_shapes=[pltpu.VMEM((GATHER_NBUF, PEER_PICKS * UV_CHUNKS, LANES), F32),
                        pltpu.SemaphoreType.DMA((GATHER_NBUF,))],
        compiler_params=_cparams(("arbitrary",), 32),
        name="peer_gather",
    )(idx_flat, gate_t, h2, x, g2, uv)


def _peer(h2, x, g2, p):
    B, T, D = x.shape
    hf = h2.reshape(B * T, D)
    idx_t, gate_t = _peer_route(hf, p)
    idx_flat = idx_t.T.reshape(-1)
    out = _peer_gather(idx_flat, gate_t, hf, x.reshape(B * T, D), g2, p["uv"], T)
    return out.reshape(B, T, D)


def _rearrange_w_in(w):
    cols = [w[:, o:o + n] for _, o, n, _ in _SEGS]
    cols.append(jnp.zeros((w.shape[0], PROJ_W - sum(n for _, _, n, _ in _SEGS)), w.dtype))
    return jnp.concatenate(cols, axis=1).astype(BF16)


def _rope_tables(S):
    t = jnp.arange(S)
    row = (t // GRID_W).astype(F32)[:, None]
    colp = (t % GRID_W).astype(F32)[:, None]
    nf = HEAD_DIM // 4
    inv = ROPE_THETA ** (-jnp.arange(nf, dtype=F32) / nf)
    ang = jnp.concatenate([row * inv, row * inv, colp * inv, colp * inv], axis=-1)
    ang = jnp.concatenate([ang, ang], axis=-1)
    return jnp.cos(ang), jnp.sin(ang)


def _layer_params(i, a):
    two = lambda g: jnp.concatenate([g, g]).reshape(1, LANES)
    hid = np.arange(LANES) // HEAD_DIM
    E = PEER_EXPERTS
    w_pq_hi, w_pq_lo = _split_bf16(a["w_pq"][i])
    keys_hi, keys_lo = _split_bf16(a["sub_keys"][i])
    return {
        "w_mod": a["w_mod"][i], "b_mod": a["b_mod"][i],
        "norm1": a["norm1"][i], "norm2": a["norm2"][i].reshape(1, D_MODEL),
        "w_in": _rearrange_w_in(a["w_in"][i]),
        "qn_a2": two(a["qn_a"][i]), "kn_a2": two(a["kn_a"][i]),
        "qn_b2": two(a["qn_b"][i]), "kn_b2": two(a["kn_b"][i]),
        "m128": jnp.asarray(hid[:, None] == hid[None, :], BF16),
        "sink_a": a["sink_a"][i], "rpb_b": a["rpb_b"][i],
        "wa2_f": a["wa2_f"][i], "wa2_b": a["wa2_b"][i],
        "ba_f": a["ba_f"][i].reshape(1, -1), "ba_b": a["ba_b"][i].reshape(1, -1),
        "gn_c": a["gn_c"][i].reshape(1, C_DV),
        "w_oa": a["w_oa"][i].astype(BF16), "w_ob": a["w_ob"][i].astype(BF16),
        "w_oc": a["w_oc"][i].astype(BF16), "w_out": a["w_out"][i].astype(BF16),
        "w_pq_hi": w_pq_hi, "w_pq_lo": w_pq_lo, "pq_g": a["pq_g"][i].reshape(1, PEER_HALF),
        "sub_keys_hi": keys_hi, "sub_keys_lo": keys_lo,
        "uv": jnp.concatenate([a["peer_u"][i].reshape(E, UV_CHUNKS // 2, LANES),
                               a["peer_v"][i].reshape(E, UV_CHUNKS // 2, LANES)],
                              axis=1).reshape(E * UV_CHUNKS, LANES),
    }


def _layer(x, ctx, c_all, p, cos, sin, last):
    B, S, D = x.shape
    L = ctx.shape[1]
    ones = jnp.ones((L, LANES), F32)
    zeros = jnp.zeros((L, LANES), F32)
    state0 = jnp.zeros((B, C_HEADS, C_DV, C_DK), F32)
    mods = _modulation(c_all, p["w_mod"], p["b_mod"])
    lat = [mods[:B, k * D:(k + 1) * D].reshape(B, 1, D) for k in range(N_MOD)]
    cm = [jnp.broadcast_to(mods[B, k * D:(k + 1) * D].reshape(1, 1, D), (B, 1, D)) for k in range(N_MOD)]
    sh1, sc1, g1, sh2, sc2, g2 = lat
    csh1, csc1, cg1, csh2, csc2, cg2 = cm

    proj = _norm_proj(x, p["norm1"], sh1, sc1, p["w_in"])
    cproj = _norm_proj(ctx, p["norm1"], csh1, csc1, p["w_in"])
    qa, ka, qb, kb = _qk_prep(proj, cos, sin, p)
    cqa, cka, cqb, ckb = _qk_prep(cproj, ones, zeros, p)

    o_a = _attn_a(qa, ka, proj, cka, cproj, p["sink_a"])
    o_b = _attn_b(qb, kb, proj, ckb, cproj, _na_bias(p["rpb_b"], S // GRID_W))
    co_f, co_b, s_f, s_b = _gla(cproj, p, state0, state0)
    o_f, o_bk, _, _ = _gla(proj, p, s_f, s_b)
    x, h2 = _merge(o_a, o_b, o_f, o_bk, proj, x, g1, sh2, sc2, p)
    x = _peer(h2, x, g2, p)
    if not last:
        co_a = _ctx_attn(cqa, cka, cproj, "vA", p["sink_a"], A_KV_HEADS, True)
        co_bb = _ctx_attn(cqb, ckb, cproj, "vB", p["sink_a"], B_HEADS, False)
        ctx, hc2 = _merge(co_a, co_bb, co_f, co_b, cproj, ctx, cg1, csh2, csc2, p)
        ctx = _peer(hc2, ctx, cg2, p)
    return x, ctx


def _forward(a):
    x, ctx = a["x"], a["ctx"]
    B, S, D = x.shape
    cos, sin = _rope_tables(S)
    c_all = jnp.zeros((8, D), F32).at[:B].set(a["c"]).at[B].set(a["c_ctx"])
    for i in range(DEPTH):
        x, ctx = _layer(x, ctx, c_all, _layer_params(i, a), cos, sin, i == DEPTH - 1)
    return x


def kernel(x, c, ctx, c_ctx, w_mod, b_mod, norm1, norm2, w_in, qn_a, kn_a, sink_a, qn_b, kn_b, rpb_b,
           wa2_f, ba_f, wa2_b, ba_b, gn_c, w_oa, w_ob, w_oc, w_out, w_pq, pq_g, sub_keys, peer_u, peer_v):
    return _forward(dict(
        x=x, c=c, ctx=ctx, c_ctx=c_ctx, w_mod=w_mod, b_mod=b_mod, norm1=norm1, norm2=norm2, w_in=w_in,
        qn_a=qn_a, kn_a=kn_a, sink_a=sink_a, qn_b=qn_b, kn_b=kn_b, rpb_b=rpb_b, wa2_f=wa2_f, ba_f=ba_f,
        wa2_b=wa2_b, ba_b=ba_b, gn_c=gn_c, w_oa=w_oa, w_ob=w_ob, w_oc=w_oc, w_out=w_out, w_pq=w_pq,
        pq_g=pq_g, sub_keys=sub_keys, peer_u=peer_u, peer_v=peer_v))
```

```python
import functools

import numpy as np
import jax
import jax.numpy as jnp
from jax import lax
from jax.experimental import pallas as pl
from jax.experimental.pallas import tpu as pltpu

F32 = jnp.float32
BF16 = jnp.bfloat16

D_MODEL = 1024
DEPTH = 4
GRID_W = 64
HEAD_DIM = 64
ROPE_THETA = 10000.0
EPS = 1e-6
NEG = -1e30
N_MOD = 6
A_HEADS = 8
A_KV_HEADS = 2
A_BLOCK = 128
B_HEADS = 8
NA_ROWS = 8
NA_COLS = 16
C_HEADS = 4
C_DK = 128
C_DV = 256
C_GATE_RANK = 16
C_GATE_TAU = 16.0
C_CHUNK = 64
PEER_HEADS = 8
PEER_NKEYS = 128
PEER_EXPERTS = PEER_NKEYS * PEER_NKEYS
PEER_HALF = 128
PEER_TOPK = 16
PEER_PICKS = PEER_HEADS * PEER_TOPK

LANES = 128
MIB = 1024 * 1024

_SEGS = (
    ("vC", 1792, 1024, 0), ("rC", 4384, 1024, 1024), ("gA", 5408, 1024, 2048),
    ("gB", 6432, 1024, 3072), ("gC", 7456, 1024, 4096), ("kB", 256, 512, 5120),
    ("vB", 768, 512, 5632), ("kC", 1280, 512, 6144), ("qA", 2848, 512, 6656),
    ("qB", 3360, 512, 7168), ("qC", 3872, 512, 7680), ("kA", 0, 128, 8192),
    ("vA", 128, 128, 8320), ("aC", 2816, 32, 8448),
)
PROJ_W = 8704
PROJ_TN = 2176
COLBLK = {name: new // (LANES if w < LANES else w) for name, _, w, new in _SEGS}
PROJ_DT = F32

NA_TR = 4
GATHER_TM = 128
GATHER_NBUF = 8
UV_CHUNKS = 16


def _cparams(sem, vmem_mib):
    return pltpu.CompilerParams(dimension_semantics=sem, vmem_limit_bytes=vmem_mib * MIB)


def _split_bf16(a):
    hi = a.astype(BF16)
    lo = (a - hi.astype(F32)).astype(BF16)
    return hi, lo


def _dot(a, b):
    return jnp.dot(a, b, preferred_element_type=F32)


def _dot_nt(a, b):
    return lax.dot_general(a, b, (((1,), (1,)), ((), ())), preferred_element_type=F32)


def _dot_tn(a, b):
    return lax.dot_general(a, b, (((0,), (0,)), ((), ())), preferred_element_type=F32)


def _dot_precise(a, b):
    ah, al = _split_bf16(a)
    bh, bl = _split_bf16(b)
    return _dot(ah, bh) + _dot(ah, bl) + _dot(al, bh)


def _mod_kernel(c_ref, w_ref, b_ref, o_ref):
    c = c_ref[...]
    act = c * jax.nn.sigmoid(c)
    o_ref[...] = _dot_precise(act, w_ref[...]) + b_ref[...]


def _modulation(c_all, w_mod, b_mod):
    n = w_mod.shape[1]
    tn = D_MODEL
    return pl.pallas_call(
        _mod_kernel,
        grid=(n // tn,),
        in_specs=[pl.BlockSpec((8, D_MODEL), lambda j: (0, 0)),
                  pl.BlockSpec((D_MODEL, tn), lambda j: (0, j)),
                  pl.BlockSpec((1, tn), lambda j: (0, j))],
        out_specs=pl.BlockSpec((8, tn), lambda j: (0, j)),
        out_shape=jax.ShapeDtypeStruct((8, n), F32),
        compiler_params=_cparams(("arbitrary",), 32),
        name="modulation",
    )(c_all, w_mod, b_mod.reshape(1, n))


def _modulated_norm(x, g, sh, sc):
    ms = jnp.mean(x * x, axis=-1, keepdims=True)
    return (x * lax.rsqrt(ms + EPS) * g) * (1.0 + sc) + sh


def _norm_proj_kernel(x_ref, g_ref, sh_ref, sc_ref, w_ref, o_ref):
    h = _modulated_norm(x_ref[0], g_ref[...], sh_ref[0], sc_ref[0])
    o_ref[0] = _dot(h.astype(BF16), w_ref[...]).astype(o_ref.dtype)


def _norm_proj(x, g, sh, sc, w):
    B, T, D = x.shape
    tm = min(T, 512)
    return pl.pallas_call(
        _norm_proj_kernel,
        grid=(PROJ_W // PROJ_TN, B, T // tm),
        in_specs=[pl.BlockSpec((1, tm, D), lambda j, b, i: (b, i, 0)),
                  pl.BlockSpec((1, D), lambda j, b, i: (0, 0)),
                  pl.BlockSpec((1, 1, D), lambda j, b, i: (b, 0, 0)),
                  pl.BlockSpec((1, 1, D), lambda j, b, i: (b, 0, 0)),
                  pl.BlockSpec((D, PROJ_TN), lambda j, b, i: (0, j))],
        out_specs=pl.BlockSpec((1, tm, PROJ_TN), lambda j, b, i: (b, i, j)),
        out_shape=jax.ShapeDtypeStruct((B, T, PROJ_W), PROJ_DT),
        compiler_params=_cparams(("arbitrary", "arbitrary", "arbitrary"), 48),
        name="norm_proj",
    )(x, g.reshape(1, D), sh, sc, w)


def _head_norm(z, g2, m128):
    hi, lo = _split_bf16(z * z)
    ss = _dot(hi, m128) + _dot(lo, m128)
    return z * lax.rsqrt(ss * (1.0 / HEAD_DIM) + EPS) * g2


def _rope128(z, cos, sin, even16):
    rot = jnp.where(even16, -pltpu.roll(z, LANES - 16, 1), pltpu.roll(z, 16, 1))
    return z * cos + rot * sin


def _prep_kernel(qa_ref, qb_ref, kb_ref, ka_ref, cos_ref, sin_ref, gqa_ref, gka_ref, gqb_ref,
                 gkb_ref, m_ref, oqa_ref, oka_ref, oqb_ref, okb_ref):
    m128 = m_ref[...]
    cos = cos_ref[...]
    sin = sin_ref[...]
    lane = lax.broadcasted_iota(jnp.int32, cos.shape, 1)
    even16 = ((lane // 16) % 2) == 0
    scale = HEAD_DIM ** -0.5
    for s in range(4):
        sl = slice(s * LANES, (s + 1) * LANES)
        za = _head_norm(qa_ref[0][:, sl].astype(F32), gqa_ref[...], m128)
        oqa_ref[0, :, sl] = (_rope128(za, cos, sin, even16) * scale).astype(BF16)
        zb = _head_norm(qb_ref[0][:, sl].astype(F32), gqb_ref[...], m128)
        oqb_ref[0, :, sl] = (zb * scale).astype(BF16)
        zk = _head_norm(kb_ref[0][:, sl].astype(F32), gkb_ref[...], m128)
        okb_ref[0, :, sl] = zk.astype(BF16)
    zk = _head_norm(ka_ref[0].astype(F32), gka_ref[...], m128)
    oka_ref[0] = _rope128(zk, cos, sin, even16).astype(BF16)


def _qk_prep(proj, cos, sin, p):
    B, T, _ = proj.shape
    tm = min(T, 256)

    def col(name, w):
        blk = COLBLK[name]
        return pl.BlockSpec((1, tm, w), lambda b, i: (b, i, blk))

    vec = pl.BlockSpec((1, LANES), lambda b, i: (0, 0))
    tab = pl.BlockSpec((tm, LANES), lambda b, i: (i, 0))
    out = lambda w: pl.BlockSpec((1, tm, w), lambda b, i: (b, i, 0))
    return pl.pallas_call(
        _prep_kernel,
        grid=(B, T // tm),
        in_specs=[col("qA", 512), col("qB", 512), col("kB", 512), col("kA", 128), tab, tab,
                  vec, vec, vec, vec, pl.BlockSpec((LANES, LANES), lambda b, i: (0, 0))],
        out_specs=[out(512), out(128), out(512), out(512)],
        out_shape=[jax.ShapeDtypeStruct((B, T, 512), BF16), jax.ShapeDtypeStruct((B, T, 128), BF16),
                   jax.ShapeDtypeStruct((B, T, 512), BF16), jax.ShapeDtypeStruct((B, T, 512), BF16)],
        compiler_params=_cparams(("arbitrary", "arbitrary"), 32),
        name="qk_prep",
    )(proj, proj, proj, proj, cos, sin, p["qn_a2"], p["kn_a2"], p["qn_b2"], p["kn_b2"], p["m128"])


def _softmax_pv(parts, sink):
    m = parts[0][0].max(axis=-1, keepdims=True)
    for s, _ in parts[1:]:
        m = jnp.maximum(m, s.max(axis=-1, keepdims=True))
    if sink is not None:
        m = jnp.maximum(m, sink)
    den = None
    acc = None
    for s, v in parts:
        e = jnp.exp(s - m)
        ds = e.sum(axis=-1, keepdims=True)
        den = ds if den is None else den + ds
        pv = _dot(e.astype(BF16), v)
        acc = pv if acc is None else acc + pv
    if sink is not None:
        den = den + jnp.exp(sink - m)
    return acc / den


def _attn_a_kernel(sink_ref, q_ref, kp_ref, ko_ref, kn_ref, vp_ref, vo_ref, vn_ref, ck_ref, cv_ref, o_ref):
    n = pl.program_id(1)
    nb = pl.num_programs(1)
    q = q_ref[0]
    k_loc = jnp.concatenate([kp_ref[0], ko_ref[0], kn_ref[0]], axis=0)
    v_loc = jnp.concatenate([vp_ref[0], vo_ref[0], vn_ref[0]], axis=0).astype(BF16)
    ck = ck_ref[0]
    cv = cv_ref[0].astype(BF16)
    qi = lax.broadcasted_iota(jnp.int32, (A_BLOCK, 3 * A_BLOCK), 0)
    kc = lax.broadcasted_iota(jnp.int32, (A_BLOCK, 3 * A_BLOCK), 1)
    rel = kc - qi
    valid = (rel >= 0) & (rel <= 2 * A_BLOCK)
    valid = valid & ((n > 0) | (kc >= A_BLOCK)) & ((n < nb - 1) | (kc < 2 * A_BLOCK))
    rep = A_HEADS // A_KV_HEADS
    for h in range(A_HEADS):
        g = h // rep
        qh = q[:, h * HEAD_DIM:(h + 1) * HEAD_DIM]
        gs = slice(g * HEAD_DIM, (g + 1) * HEAD_DIM)
        s_loc = jnp.where(valid, _dot_nt(qh, k_loc[:, gs]), NEG)
        s_ctx = _dot_nt(qh, ck[:, gs])
        o = _softmax_pv([(s_loc, v_loc[:, gs]), (s_ctx, cv[:, gs])], sink_ref[h])
        o_ref[0, :, h * HEAD_DIM:(h + 1) * HEAD_DIM] = o.astype(o_ref.dtype)


def _attn_a(qa, ka, proj, cka, cproj, sink):
    B, S, _ = qa.shape
    L = cka.shape[1]
    nb = S // A_BLOCK
    vblk = COLBLK["vA"]
    blk = lambda f: pl.BlockSpec((1, A_BLOCK, 128), f)
    return pl.pallas_call(
        _attn_a_kernel,
        grid=(B, nb),
        in_specs=[pl.BlockSpec(memory_space=pltpu.SMEM),
                  pl.BlockSpec((1, A_BLOCK, 512), lambda b, n: (b, n, 0)),
                  blk(lambda b, n: (b, jnp.maximum(n - 1, 0), 0)),
                  blk(lambda b, n: (b, n, 0)),
                  blk(lambda b, n: (b, jnp.minimum(n + 1, nb - 1), 0)),
                  blk(lambda b, n: (b, jnp.maximum(n - 1, 0), vblk)),
                  blk(lambda b, n: (b, n, vblk)),
                  blk(lambda b, n: (b, jnp.minimum(n + 1, nb - 1), vblk)),
                  pl.BlockSpec((1, L, 128), lambda b, n: (b, 0, 0)),
                  pl.BlockSpec((1, L, 128), lambda b, n: (b, 0, vblk))],
        out_specs=pl.BlockSpec((1, A_BLOCK, 512), lambda b, n: (b, n, 0)),
        out_shape=jax.ShapeDtypeStruct((B, S, 512), BF16),
        compiler_params=_cparams(("arbitrary", "arbitrary"), 32),
        name="attn_window",
    )(sink, qa, ka, ka, ka, proj, proj, proj, cka, cproj)


def _attn_b_kernel(q_ref, k0_ref, k1_ref, k2_ref, v0_ref, v1_ref, v2_ref, ck_ref, cv_ref, bias_ref, o_ref):
    q = q_ref[0]
    k_loc = jnp.concatenate([k0_ref[0], k1_ref[0], k2_ref[0]], axis=0)
    v_loc = jnp.concatenate([v0_ref[0], v1_ref[0], v2_ref[0]], axis=0).astype(BF16)
    ck = ck_ref[0]
    cv = cv_ref[0].astype(BF16)
    for h in range(B_HEADS):
        hs = slice(h * HEAD_DIM, (h + 1) * HEAD_DIM)
        qh = q[:, hs]
        s_loc = _dot_nt(qh, k_loc[:, hs]) + bias_ref[0, h]
        s_ctx = _dot_nt(qh, ck[:, hs])
        o = _softmax_pv([(s_loc, v_loc[:, hs]), (s_ctx, cv[:, hs])], None)
        o_ref[0, :, hs] = o.astype(o_ref.dtype)


def _na_bias(rpb, rows):
    c = np.arange(GRID_W)
    col_start = np.clip(c - NA_COLS // 2, 0, GRID_W - NA_COLS)
    col_valid = (c[None, :] >= col_start[:, None]) & (c[None, :] < col_start[:, None] + NA_COLS)
    dc = np.clip(c[None, :] - c[:, None], -(NA_COLS - 1), NA_COLS - 1) + (NA_COLS - 1)
    toe = jnp.where(jnp.asarray(col_valid)[None, None], jnp.take(rpb, jnp.asarray(dc.reshape(-1)), axis=2)
                    .reshape(B_HEADS, 2 * NA_ROWS - 1, GRID_W, GRID_W), NEG)
    masked = jnp.full((B_HEADS, GRID_W, GRID_W), NEG, F32)
    tiles = []
    for r0 in (0, NA_TR, rows - NA_TR):
        band = []
        for rq in range(r0, r0 + NA_TR):
            start = min(max(rq - NA_ROWS // 2, 0), rows - NA_ROWS)
            blocks = [toe[:, kr - rq + NA_ROWS - 1] if start <= kr < start + NA_ROWS else masked
                      for kr in range(r0 - NA_TR, r0 + 2 * NA_TR)]
            band.append(jnp.concatenate(blocks, axis=2))
        tiles.append(jnp.concatenate(band, axis=1))
    return jnp.stack(tiles).astype(F32)


def _attn_b(qb, kb, proj, ckb, cproj, bias):
    B, S, _ = qb.shape
    L = ckb.shape[1]
    tq = NA_TR * GRID_W
    ns = S // tq
    vblk = COLBLK["vB"]
    blk = lambda f: pl.BlockSpec((1, tq, 512), f)
    lo = lambda b, i: (b, jnp.maximum(i - 1, 0), 0)
    hi = lambda b, i: (b, jnp.minimum(i + 1, ns - 1), 0)
    variant = lambda b, i: (jnp.where(i == 0, 0, jnp.where(i == ns - 1, 2, 1)), 0, 0, 0)
    return pl.pallas_call(
        _attn_b_kernel,
        grid=(B, ns),
        in_specs=[blk(lambda b, i: (b, i, 0)),
                  blk(lo), blk(lambda b, i: (b, i, 0)), blk(hi),
                  blk(lambda b, i: lo(b, i)[:2] + (vblk,)),
                  blk(lambda b, i: (b, i, vblk)),
                  blk(lambda b, i: hi(b, i)[:2] + (vblk,)),
                  pl.BlockSpec((1, L, 512), lambda b, i: (b, 0, 0)),
                  pl.BlockSpec((1, L, 512), lambda b, i: (b, 0, vblk)),
                  pl.BlockSpec((1, B_HEADS, tq, 3 * tq), variant)],
        out_specs=blk(lambda b, i: (b, i, 0)),
        out_shape=jax.ShapeDtypeStruct((B, S, 512), BF16),
        compiler_params=_cparams(("arbitrary", "arbitrary"), 56),
        name="attn_neighbourhood",
    )(qb, kb, kb, kb, proj, proj, proj, ckb, cproj, bias)


def _ctx_attn_kernel(sink_ref, q_ref, k_ref, v_ref, o_ref, *, kv_heads, use_sink):
    q = q_ref[0]
    k = k_ref[0]
    v = v_ref[0].astype(BF16)
    rep = A_HEADS // kv_heads
    for h in range(A_HEADS):
        g = h // rep
        gs = slice(g * HEAD_DIM, (g + 1) * HEAD_DIM)
        s = _dot_nt(q[:, h * HEAD_DIM:(h + 1) * HEAD_DIM], k[:, gs])
        o = _softmax_pv([(s, v[:, gs])], sink_ref[h] if use_sink else None)
        o_ref[0, :, h * HEAD_DIM:(h + 1) * HEAD_DIM] = o.astype(o_ref.dtype)


def _ctx_attn(q, k, cproj, vname, sink, kv_heads, use_sink):
    B, L, _ = q.shape
    kw = kv_heads * HEAD_DIM
    vblk = COLBLK[vname]
    return pl.pallas_call(
        functools.partial(_ctx_attn_kernel, kv_heads=kv_heads, use_sink=use_sink),
        grid=(B,),
        in_specs=[pl.BlockSpec(memory_space=pltpu.SMEM),
                  pl.BlockSpec((1, L, 512), lambda b: (b, 0, 0)),
                  pl.BlockSpec((1, L, kw), lambda b: (b, 0, 0)),
                  pl.BlockSpec((1, L, kw), lambda b: (b, 0, vblk))],
        out_specs=pl.BlockSpec((1, L, 512), lambda b: (b, 0, 0)),
        out_shape=jax.ShapeDtypeStruct((B, L, 512), BF16),
        compiler_params=_cparams(("arbitrary",), 32),
        name="attn_context",
    )(sink, q, k, cproj)


def _gla_direction(q_ref, k_ref, v_ref, a_ref, w2_ref, b_ref, o_ref, S, forward):
    C = q_ref.shape[1]
    lo = 0 if forward else C_GATE_RANK
    a = a_ref[0][:, lo:lo + C_GATE_RANK].astype(F32)
    z = _dot_precise(a, w2_ref[...]) + b_ref[...]
    la = (jnp.minimum(z, 0.0) - jnp.log1p(jnp.exp(-jnp.abs(z)))) * (1.0 / C_GATE_TAU)
    ri = lax.broadcasted_iota(jnp.int32, (C, C), 0)
    ci = lax.broadcasted_iota(jnp.int32, (C, C), 1)
    keep = (ci <= ri) if forward else (ci >= ri)
    tri = keep.astype(BF16)
    la_hi, la_lo = _split_bf16(la)
    cum = _dot(tri, la_hi) + _dot(tri, la_lo)
    ref_row = C // 2 - 1 if forward else C // 2
    end_row = C - 1 if forward else 0
    scale = C_DK ** -0.5
    for h in range(C_HEADS):
        ks = slice(h * C_DK, (h + 1) * C_DK)
        vs = slice(h * C_DV, (h + 1) * C_DV)
        c = cum[:, ks]
        c_ref = c[ref_row:ref_row + 1]
        c_end = c[end_row:end_row + 1]
        q = q_ref[0][:, ks].astype(F32) * scale
        k = k_ref[0][:, ks].astype(F32)
        v = v_ref[0][:, vs].astype(BF16)
        att = _dot_nt((q * jnp.exp(c - c_ref)).astype(BF16), (k * jnp.exp(c_ref - c)).astype(BF16))
        att = jnp.where(keep, att, 0.0)
        st = S[h]
        o = _dot(att.astype(BF16), v) + _dot_nt((q * jnp.exp(c)).astype(BF16), st.astype(BF16))
        kd = (k * jnp.exp(c_end - c)).astype(BF16)
        S[h] = st * jnp.exp(c_end) + _dot_tn(v, kd)
        o_ref[0, :, vs] = o


def _gla_kernel(qf, kf, vf, af, qb, kb, vb, ab, w2f, w2b, bf, bb, s0f, s0b, of, ob, sff, sfb, Sf, Sb):
    j = pl.program_id(1)

    @pl.when(j == 0)
    def _():
        Sf[...] = s0f[0]
        Sb[...] = s0b[0]

    _gla_direction(qf, kf, vf, af, w2f, bf, of, Sf, True)
    _gla_direction(qb, kb, vb, ab, w2b, bb, ob, Sb, False)

    @pl.when(j == pl.num_programs(1) - 1)
    def _():
        sff[0] = Sf[...]
        sfb[0] = Sb[...]


def _gla(proj, p, s0f, s0b):
    B, T, _ = proj.shape
    C = C_CHUNK
    n = T // C
    fwd = lambda b, j: (b, j)
    bwd = lambda b, j: (b, n - 1 - j)

    def cols(order):
        mk = lambda name, w: pl.BlockSpec((1, C, w), lambda b, j: order(b, j) + (COLBLK[name],))
        return [mk("qC", 512), mk("kC", 512), mk("vC", 1024), mk("aC", 128)]

    wspec = pl.BlockSpec((C_GATE_RANK, 512), lambda b, j: (0, 0))
    bspec = pl.BlockSpec((1, 512), lambda b, j: (0, 0))
    sspec = pl.BlockSpec((1, C_HEADS, C_DV, C_DK), lambda b, j: (b, 0, 0, 0))
    st = jax.ShapeDtypeStruct((B, C_HEADS, C_DV, C_DK), F32)
    return pl.pallas_call(
        _gla_kernel,
        grid=(B, n),
        in_specs=cols(fwd) + cols(bwd) + [wspec, wspec, bspec, bspec, sspec, sspec],
        out_specs=[pl.BlockSpec((1, C, 1024), lambda b, j: (b, j, 0)),
                   pl.BlockSpec((1, C, 1024), lambda b, j: (b, n - 1 - j, 0)),
                   sspec, sspec],
        out_shape=[jax.ShapeDtypeStruct((B, T, 1024), F32), jax.ShapeDtypeStruct((B, T, 1024), F32), st, st],
        scratch_shapes=[pltpu.VMEM((C_HEADS, C_DV, C_DK), F32), pltpu.VMEM((C_HEADS, C_DV, C_DK), F32)],
        compiler_params=_cparams(("arbitrary", "arbitrary"), 32),
        name="gla_scan",
    )(proj, proj, proj, proj, proj, proj, proj, proj, p["wa2_f"], p["wa2_b"], p["ba_f"], p["ba_b"], s0f, s0b)


def _merge_kernel(oa_ref, ob_ref, of_ref, obk_ref, r_ref, ga_ref, gb_ref, gc_ref, x_ref, g1_ref, sh_ref,
                  sc_ref, n2_ref, gn_ref, woa_ref, wob_ref, woc_ref, wout_ref, xo_ref, h2_ref):
    o = of_ref[0] + obk_ref[0]
    r = r_ref[0].astype(F32)
    parts = []
    for h in range(C_HEADS):
        vs = slice(h * C_DV, (h + 1) * C_DV)
        oh = o[:, vs]
        ms = jnp.mean(oh * oh, axis=-1, keepdims=True)
        rh = r[:, vs]
        parts.append((oh * lax.rsqrt(ms + EPS) * gn_ref[...]) * (rh * jax.nn.sigmoid(rh)))
    oc = jnp.concatenate(parts, axis=-1).astype(BF16)
    y = (jax.nn.sigmoid(ga_ref[0].astype(F32)) * _dot(oa_ref[0], woa_ref[...])
         + jax.nn.sigmoid(gb_ref[0].astype(F32)) * _dot(ob_ref[0], wob_ref[...])
         + jax.nn.sigmoid(gc_ref[0].astype(F32)) * _dot(oc, woc_ref[...]))
    xn = x_ref[0] + g1_ref[0] * _dot(y.astype(BF16), wout_ref[...])
    xo_ref[0] = xn
    h2_ref[0] = _modulated_norm(xn, n2_ref[...], sh_ref[0], sc_ref[0])


def _merge(oa, ob, of, obk, proj, x, g1, sh2, sc2, p):
    B, T, D = x.shape
    tm = min(T, 256)
    tok = lambda w: pl.BlockSpec((1, tm, w), lambda b, i: (b, i, 0))
    col = lambda name: pl.BlockSpec((1, tm, 1024), lambda b, i: (b, i, COLBLK[name]))
    mod = pl.BlockSpec((1, 1, D), lambda b, i: (b, 0, 0))
    full = lambda r, c: pl.BlockSpec((r, c), lambda b, i: (0, 0))
    return pl.pallas_call(
        _merge_kernel,
        grid=(B, T // tm),
        in_specs=[tok(512), tok(512), tok(1024), tok(1024), col("rC"), col("gA"), col("gB"), col("gC"),
                  tok(D), mod, mod, mod, full(1, D), full(1, C_DV),
                  full(512, D), full(512, D), full(1024, D), full(D, D)],
        out_specs=[tok(D), tok(D)],
        out_shape=[jax.ShapeDtypeStruct((B, T, D), F32), jax.ShapeDtypeStruct((B, T, D), F32)],
        compiler_params=_cparams(("arbitrary", "arbitrary"), 56),
        name="merge",
    )(oa, ob, of, obk, proj, proj, proj, proj, x, g1, sh2, sc2, p["norm2"], p["gn_c"],
      p["w_oa"], p["w_ob"], p["w_oc"], p["w_out"])


def _topk_rows(arr, rowid, k, payload=None):
    n = arr.shape[0]
    vals, ids = [], []
    for _ in range(k):
        m = arr.max(axis=0, keepdims=True)
        sel = jnp.where(arr == m, rowid, float(n)).min(axis=0, keepdims=True)
        hit = rowid == sel
        vals.append(m)
        ids.append(sel if payload is None else jnp.where(hit, payload, -1.0).max(axis=0, keepdims=True))
        arr = jnp.where(hit, -jnp.inf, arr)
    return jnp.concatenate(vals, axis=0), jnp.concatenate(ids, axis=0)


_PAIR_WIDTH = tuple(PEER_TOPK // (a + 1) for a in range(PEER_TOPK))


def _route_kernel(h_ref, wh_ref, wl_ref, g_ref, kh_ref, kl_ref, idx_ref, gate_ref):
    tm = h_ref.shape[0]
    hh, hl = _split_bf16(h_ref[...])
    q = _dot(hh, wh_ref[...]) + _dot(hh, wl_ref[...]) + _dot(hl, wh_ref[...])
    rowid = lax.broadcasted_iota(jnp.int32, (PEER_NKEYS, tm), 0).astype(F32)
    pos = lax.broadcasted_iota(jnp.int32, (sum(_PAIR_WIDTH), tm), 0).astype(F32)
    for p in range(PEER_HEADS):
        tops = []
        for half in range(2):
            seg = q[:, (2 * p + half) * PEER_HALF:(2 * p + half + 1) * PEER_HALF]
            ms = jnp.mean(seg * seg, axis=-1, keepdims=True)
            qh, ql = _split_bf16(seg * lax.rsqrt(ms + EPS) * g_ref[...])
            kh, kl = kh_ref[p, half], kl_ref[p, half]
            s_t = _dot_nt(kh, qh) + _dot_nt(kh, ql) + _dot_nt(kl, qh)
            tops.append(_topk_rows(s_t, rowid, PEER_TOPK))
        (s1, i1), (s2, i2) = tops
        cand = jnp.concatenate([s1[a:a + 1] + s2[:w] for a, w in enumerate(_PAIR_WIDTH)], axis=0)
        cidx = jnp.concatenate([i1[a:a + 1] * float(PEER_NKEYS) + i2[:w] for a, w in enumerate(_PAIR_WIDTH)],
                               axis=0)
        top, eid = _topk_rows(cand, pos, PEER_TOPK, payload=cidx)
        e = jnp.exp(top - top[0:1])
        rs = slice(p * PEER_TOPK, (p + 1) * PEER_TOPK)
        idx_ref[rs, :] = eid.astype(jnp.int32)
        gate_ref[rs, :] = e / e.sum(axis=0, keepdims=True)


def _peer_route(h2, p):
    T, D = h2.shape
    tm = min(T, 256)
    wspec = pl.BlockSpec((D, 2 * PEER_HEADS * PEER_HALF), lambda i: (0, 0))
    kspec = pl.BlockSpec((PEER_HEADS, 2, PEER_NKEYS, PEER_HALF), lambda i: (0, 0, 0, 0))
    return pl.pallas_call(
        _route_kernel,
        grid=(T // tm,),
        in_specs=[pl.BlockSpec((tm, D), lambda i: (i, 0)), wspec, wspec,
                  pl.BlockSpec((1, PEER_HALF), lambda i: (0, 0)), kspec, kspec],
        out_specs=[pl.BlockSpec((PEER_PICKS, tm), lambda i: (0, i)),
                   pl.BlockSpec((PEER_PICKS, tm), lambda i: (0, i))],
        out_shape=[jax.ShapeDtypeStruct((PEER_PICKS, T), jnp.int32),
                   jax.ShapeDtypeStruct((PEER_PICKS, T), F32)],
        compiler_params=_cparams(("arbitrary",), 48),
        name="peer_route",
    )(h2, p["w_pq_hi"], p["w_pq_lo"], p["pq_g"], p["sub_keys_hi"], p["sub_keys_lo"])


def _gather_kernel(idx_ref, nidx_ref, gate_ref, h_ref, x_ref, g2_ref, uv_hbm, o_ref, buf, sem):
    step = pl.program_id(0)
    tm = h_ref.shape[0]
    nbuf = buf.shape[0]
    ahead = nbuf - 1

    def issue(ref, t, slot):
        for k in range(PEER_PICKS):
            row = pl.multiple_of(ref[t * PEER_PICKS + k] * UV_CHUNKS, UV_CHUNKS)
            pltpu.make_async_copy(uv_hbm.at[pl.ds(row, UV_CHUNKS)],
                                  buf.at[slot, pl.ds(k * UV_CHUNKS, UV_CHUNKS)], sem.at[slot]).start()

    def wait(slot):
        pltpu.make_async_copy(uv_hbm.at[pl.ds(0, PEER_PICKS * UV_CHUNKS)], buf.at[slot], sem.at[slot]).wait()

    @pl.when(step == 0)
    def _():
        for t0 in range(ahead):
            issue(idx_ref, t0, t0)

    lane = lax.broadcasted_iota(jnp.int32, (PEER_PICKS, tm), 1)
    half = UV_CHUNKS // 2

    def token(t, slot, h_row):
        wait(slot)
        acc = None
        for c in range(half):
            u_c = buf[slot, pl.ds(c, PEER_PICKS, stride=UV_CHUNKS), :]
            term = u_c * h_row[:, c * LANES:(c + 1) * LANES]
            acc = term if acc is None else acc + term
        d = acc.sum(axis=1, keepdims=True)
        act = 0.5 * d * (1.0 + lax.erf(d * (2.0 ** -0.5)))
        gcol = jnp.where(lane == t, gate_ref[...], 0.0).sum(axis=1, keepdims=True)
        w = gcol * act
        mix = []
        for c in range(half):
            v_c = buf[slot, pl.ds(half + c, PEER_PICKS, stride=UV_CHUNKS), :]
            mix.append((v_c * w).sum(axis=0, keepdims=True))
        return jnp.concatenate(mix, axis=1)

    def group(t0, final):
        h_rows = h_ref[pl.ds(t0, nbuf), :]
        outs = []
        for j in range(nbuf):
            if final and j > 0:
                issue(nidx_ref, j - 1, j - 1)
            else:
                issue(idx_ref, t0 + j + ahead, (j + ahead) % nbuf)
            outs.append(token(t0 + j, j, h_rows[j:j + 1]))
        o_ref[pl.ds(t0, nbuf), :] = x_ref[pl.ds(t0, nbuf), :] + g2_ref[0] * jnp.concatenate(outs, axis=0)

    def body(g, carry):
        group(pl.multiple_of(g * nbuf, nbuf), False)
        return carry

    lax.fori_loop(0, tm // nbuf - 1, body, 0)
    group(tm - nbuf, True)

    @pl.when(step == pl.num_programs(0) - 1)
    def _():
        for s in range(ahead):
            wait(s)


def _peer_gather(idx_flat, gate_t, h2, x, g2, uv, tokens_per_batch):
    T, D = h2.shape
    tm = GATHER_TM
    n = T // tm
    head = GATHER_NBUF * PEER_PICKS
    return pl.pallas_call(
        _gather_kernel,
        grid=(n,),
        in_specs=[pl.BlockSpec((tm * PEER_PICKS,), lambda i: (i,), memory_space=pltpu.SMEM),
                  pl.BlockSpec((head,), lambda i: (jnp.minimum(i + 1, n - 1) * (tm // GATHER_NBUF),),
                               memory_space=pltpu.SMEM),
                  pl.BlockSpec((PEER_PICKS, tm), lambda i: (0, i)),
                  pl.BlockSpec((tm, D), lambda i: (i, 0)),
                  pl.BlockSpec((tm, D), lambda i: (i, 0)),
                  pl.BlockSpec((1, 1, D), lambda i: ((i * tm) // tokens_per_batch, 0, 0)),
                  pl.BlockSpec(memory_space=pl.ANY)],
        out_specs=pl.BlockSpec((tm, D), lambda i: (i, 0)),
        out_shape=jax.ShapeDtypeStruct((T, D), F32),
        scratch_shapes=[pltpu.VMEM((GATHER_NBUF, PEER_PICKS * UV_CHUNKS, LANES), F32),
                        pltpu.SemaphoreType.DMA((GATHER_NBUF,))],
        compiler_params=_cparams(("arbitrary",), 32),
        name="peer_gather",
    )(idx_flat, idx_flat, gate_t, h2, x, g2, uv)


def _peer(h2, x, g2, p):
    B, T, D = x.shape
    hf = h2.reshape(B * T, D)
    idx_t, gate_t = _peer_route(hf, p)
    idx_flat = idx_t.T.reshape(-1)
    out = _peer_gather(idx_flat, gate_t, hf, x.reshape(B * T, D), g2, p["uv"], T)
    return out.reshape(B, T, D)


def _rearrange_w_in(w):
    cols = [w[:, o:o + n] for _, o, n, _ in _SEGS]
    cols.append(jnp.zeros((w.shape[0], PROJ_W - sum(n for _, _, n, _ in _SEGS)), w.dtype))
    return jnp.concatenate(cols, axis=1).astype(BF16)


def _rope_tables(S):
    t = jnp.arange(S)
    row = (t // GRID_W).astype(F32)[:, None]
    colp = (t % GRID_W).astype(F32)[:, None]
    nf = HEAD_DIM // 4
    inv = ROPE_THETA ** (-jnp.arange(nf, dtype=F32) / nf)
    ang = jnp.concatenate([row * inv, row * inv, colp * inv, colp * inv], axis=-1)
    ang = jnp.concatenate([ang, ang], axis=-1)
    return jnp.cos(ang), jnp.sin(ang)


def _layer_params(i, a):
    two = lambda g: jnp.concatenate([g, g]).reshape(1, LANES)
    hid = np.arange(LANES) // HEAD_DIM
    E = PEER_EXPERTS
    w_pq_hi, w_pq_lo = _split_bf16(a["w_pq"][i])
    keys_hi, keys_lo = _split_bf16(a["sub_keys"][i])
    return {
        "w_mod": a["w_mod"][i], "b_mod": a["b_mod"][i],
        "norm1": a["norm1"][i], "norm2": a["norm2"][i].reshape(1, D_MODEL),
        "w_in": _rearrange_w_in(a["w_in"][i]),
        "qn_a2": two(a["qn_a"][i]), "kn_a2": two(a["kn_a"][i]),
        "qn_b2": two(a["qn_b"][i]), "kn_b2": two(a["kn_b"][i]),
        "m128": jnp.asarray(hid[:, None] == hid[None, :], BF16),
        "sink_a": a["sink_a"][i], "rpb_b": a["rpb_b"][i],
        "wa2_f": a["wa2_f"][i], "wa2_b": a["wa2_b"][i],
        "ba_f": a["ba_f"][i].reshape(1, -1), "ba_b": a["ba_b"][i].reshape(1, -1),
        "gn_c": a["gn_c"][i].reshape(1, C_DV),
        "w_oa": a["w_oa"][i].astype(BF16), "w_ob": a["w_ob"][i].astype(BF16),
        "w_oc": a["w_oc"][i].astype(BF16), "w_out": a["w_out"][i].astype(BF16),
        "w_pq_hi": w_pq_hi, "w_pq_lo": w_pq_lo, "pq_g": a["pq_g"][i].reshape(1, PEER_HALF),
        "sub_keys_hi": keys_hi, "sub_keys_lo": keys_lo,
        "uv": jnp.concatenate([a["peer_u"][i].reshape(E, UV_CHUNKS // 2, LANES),
                               a["peer_v"][i].reshape(E, UV_CHUNKS // 2, LANES)],
                              axis=1).reshape(E * UV_CHUNKS, LANES),
    }


def _layer(x, ctx, c_all, p, cos, sin, last):
    B, S, D = x.shape
    L = ctx.shape[1]
    ones = jnp.ones((L, LANES), F32)
    zeros = jnp.zeros((L, LANES), F32)
    state0 = jnp.zeros((B, C_HEADS, C_DV, C_DK), F32)
    mods = _modulation(c_all, p["w_mod"], p["b_mod"])
    lat = [mods[:B, k * D:(k + 1) * D].reshape(B, 1, D) for k in range(N_MOD)]
    cm = [jnp.broadcast_to(mods[B, k * D:(k + 1) * D].reshape(1, 1, D), (B, 1, D)) for k in range(N_MOD)]
    sh1, sc1, g1, sh2, sc2, g2 = lat
    csh1, csc1, cg1, csh2, csc2, cg2 = cm

    proj = _norm_proj(x, p["norm1"], sh1, sc1, p["w_in"])
    cproj = _norm_proj(ctx, p["norm1"], csh1, csc1, p["w_in"])
    qa, ka, qb, kb = _qk_prep(proj, cos, sin, p)
    cqa, cka, cqb, ckb = _qk_prep(cproj, ones, zeros, p)

    o_a = _attn_a(qa, ka, proj, cka, cproj, p["sink_a"])
    o_b = _attn_b(qb, kb, proj, ckb, cproj, _na_bias(p["rpb_b"], S // GRID_W))
    co_f, co_b, s_f, s_b = _gla(cproj, p, state0, state0)
    o_f, o_bk, _, _ = _gla(proj, p, s_f, s_b)
    x, h2 = _merge(o_a, o_b, o_f, o_bk, proj, x, g1, sh2, sc2, p)
    x = _peer(h2, x, g2, p)
    if not last:
        co_a = _ctx_attn(cqa, cka, cproj, "vA", p["sink_a"], A_KV_HEADS, True)
        co_bb = _ctx_attn(cqb, ckb, cproj, "vB", p["sink_a"], B_HEADS, False)
        ctx, hc2 = _merge(co_a, co_bb, co_f, co_b, cproj, ctx, cg1, csh2, csc2, p)
        ctx = _peer(hc2, ctx, cg2, p)
    return x, ctx


def _forward(a):
    x, ctx = a["x"], a["ctx"]
    B, S, D = x.shape
    cos, sin = _rope_tables(S)
    c_all = jnp.zeros((8, D), F32).at[:B].set(a["c"]).at[B].set(a["c_ctx"])
    for i in range(DEPTH):
        x, ctx = _layer(x, ctx, c_all, _layer_params(i, a), cos, sin, i == DEPTH - 1)
    return x


def kernel(x, c, ctx, c_ctx, w_mod, b_mod, norm1, norm2, w_in, qn_a, kn_a, sink_a, qn_b, kn_b, rpb_b,
           wa2_f, ba_f, wa2_b, ba_b, gn_c, w_oa, w_ob, w_oc, w_out, w_pq, pq_g, sub_keys, peer_u, peer_v):
    return _forward(dict(
        x=x, c=c, ctx=ctx, c_ctx=c_ctx, w_mod=w_mod, b_mod=b_mod, norm1=norm1, norm2=norm2, w_in=w_in,
        qn_a=qn_a, kn_a=kn_a, sink_a=sink_a, qn_b=qn_b, kn_b=kn_b, rpb_b=rpb_b, wa2_f=wa2_f, ba_f=ba_f,
        wa2_b=wa2_b, ba_b=ba_b, gn_c=gn_c, w_oa=w_oa, w_ob=w_ob, w_oc=w_oc, w_out=w_out, w_pq=w_pq,
        pq_g=pq_g, sub_keys=sub_keys, peer_u=peer_u, peer_v=peer_v))
```

```python
import functools

import numpy as np
import jax
import jax.numpy as jnp
from jax import lax
from jax.experimental import pallas as pl
from jax.experimental.pallas import tpu as pltpu

F32 = jnp.float32
BF16 = jnp.bfloat16

D_MODEL = 1024
DEPTH = 4
GRID_W = 64
HEAD_DIM = 64
ROPE_THETA = 10000.0
EPS = 1e-6
NEG = -1e30
N_MOD = 6
A_HEADS = 8
A_KV_HEADS = 2
A_BLOCK = 128
B_HEADS = 8
NA_ROWS = 8
NA_COLS = 16
C_HEADS = 4
C_DK = 128
C_DV = 256
C_GATE_RANK = 16
C_GATE_TAU = 16.0
C_CHUNK = 64
PEER_HEADS = 8
PEER_NKEYS = 128
PEER_EXPERTS = PEER_NKEYS * PEER_NKEYS
PEER_HALF = 128
PEER_TOPK = 16
PEER_PICKS = PEER_HEADS * PEER_TOPK

LANES = 128
MIB = 1024 * 1024

_SEGS = (
    ("vC", 1792, 1024, 0), ("rC", 4384, 1024, 1024), ("gA", 5408, 1024, 2048),
    ("gB", 6432, 1024, 3072), ("gC", 7456, 1024, 4096), ("kB", 256, 512, 5120),
    ("vB", 768, 512, 5632), ("kC", 1280, 512, 6144), ("qA", 2848, 512, 6656),
    ("qB", 3360, 512, 7168), ("qC", 3872, 512, 7680), ("kA", 0, 128, 8192),
    ("vA", 128, 128, 8320), ("aC", 2816, 32, 8448),
)
PROJ_W = 8704
PROJ_TN = 2176
COLBLK = {name: new // (LANES if w < LANES else w) for name, _, w, new in _SEGS}
PROJ_DT = F32

NA_TR = 4
GATHER_TM = 128
GATHER_NBUF = 8
UV_CHUNKS = 16


def _cparams(sem, vmem_mib):
    return pltpu.CompilerParams(dimension_semantics=sem, vmem_limit_bytes=vmem_mib * MIB)


def _split_bf16(a):
    hi = a.astype(BF16)
    lo = (a - hi.astype(F32)).astype(BF16)
    return hi, lo


def _dot(a, b):
    return jnp.dot(a, b, preferred_element_type=F32)


def _dot_nt(a, b):
    return lax.dot_general(a, b, (((1,), (1,)), ((), ())), preferred_element_type=F32)


def _dot_tn(a, b):
    return lax.dot_general(a, b, (((0,), (0,)), ((), ())), preferred_element_type=F32)


def _dot_precise(a, b):
    ah, al = _split_bf16(a)
    bh, bl = _split_bf16(b)
    return _dot(ah, bh) + _dot(ah, bl) + _dot(al, bh)


def _mod_kernel(c_ref, w_ref, b_ref, o_ref):
    c = c_ref[...]
    act = c * jax.nn.sigmoid(c)
    o_ref[...] = _dot_precise(act, w_ref[...]) + b_ref[...]


def _modulation(c_all, w_mod, b_mod):
    n = w_mod.shape[1]
    tn = D_MODEL
    return pl.pallas_call(
        _mod_kernel,
        grid=(n // tn,),
        in_specs=[pl.BlockSpec((8, D_MODEL), lambda j: (0, 0)),
                  pl.BlockSpec((D_MODEL, tn), lambda j: (0, j)),
                  pl.BlockSpec((1, tn), lambda j: (0, j))],
        out_specs=pl.BlockSpec((8, tn), lambda j: (0, j)),
        out_shape=jax.ShapeDtypeStruct((8, n), F32),
        compiler_params=_cparams(("arbitrary",), 32),
        name="modulation",
    )(c_all, w_mod, b_mod.reshape(1, n))


def _modulated_norm(x, g, sh, sc):
    ms = jnp.mean(x * x, axis=-1, keepdims=True)
    return (x * lax.rsqrt(ms + EPS) * g) * (1.0 + sc) + sh


def _norm_proj_kernel(x_ref, g_ref, sh_ref, sc_ref, w_ref, o_ref):
    h = _modulated_norm(x_ref[0], g_ref[...], sh_ref[0], sc_ref[0])
    o_ref[0] = _dot(h.astype(BF16), w_ref[...]).astype(o_ref.dtype)


def _norm_proj(x, g, sh, sc, w):
    B, T, D = x.shape
    tm = min(T, 512)
    return pl.pallas_call(
        _norm_proj_kernel,
        grid=(PROJ_W // PROJ_TN, B, T // tm),
        in_specs=[pl.BlockSpec((1, tm, D), lambda j, b, i: (b, i, 0)),
                  pl.BlockSpec((1, D), lambda j, b, i: (0, 0)),
                  pl.BlockSpec((1, 1, D), lambda j, b, i: (b, 0, 0)),
                  pl.BlockSpec((1, 1, D), lambda j, b, i: (b, 0, 0)),
                  pl.BlockSpec((D, PROJ_TN), lambda j, b, i: (0, j))],
        out_specs=pl.BlockSpec((1, tm, PROJ_TN), lambda j, b, i: (b, i, j)),
        out_shape=jax.ShapeDtypeStruct((B, T, PROJ_W), PROJ_DT),
        compiler_params=_cparams(("arbitrary", "arbitrary", "arbitrary"), 48),
        name="norm_proj",
    )(x, g.reshape(1, D), sh, sc, w)


def _head_norm(z, g2, m128):
    hi, lo = _split_bf16(z * z)
    ss = _dot(hi, m128) + _dot(lo, m128)
    return z * lax.rsqrt(ss * (1.0 / HEAD_DIM) + EPS) * g2


def _rope128(z, cos, sin, even16):
    rot = jnp.where(even16, -pltpu.roll(z, LANES - 16, 1), pltpu.roll(z, 16, 1))
    return z * cos + rot * sin


def _prep_kernel(qa_ref, qb_ref, kb_ref, ka_ref, cos_ref, sin_ref, gqa_ref, gka_ref, gqb_ref,
                 gkb_ref, m_ref, oqa_ref, oka_ref, oqb_ref, okb_ref):
    m128 = m_ref[...]
    cos = cos_ref[...]
    sin = sin_ref[...]
    lane = lax.broadcasted_iota(jnp.int32, cos.shape, 1)
    even16 = ((lane // 16) % 2) == 0
    scale = HEAD_DIM ** -0.5
    for s in range(4):
        sl = slice(s * LANES, (s + 1) * LANES)
        za = _head_norm(qa_ref[0][:, sl].astype(F32), gqa_ref[...], m128)
        oqa_ref[0, :, sl] = (_rope128(za, cos, sin, even16) * scale).astype(BF16)
        zb = _head_norm(qb_ref[0][:, sl].astype(F32), gqb_ref[...], m128)
        oqb_ref[0, :, sl] = (zb * scale).astype(BF16)
        zk = _head_norm(kb_ref[0][:, sl].astype(F32), gkb_ref[...], m128)
        okb_ref[0, :, sl] = zk.astype(BF16)
    zk = _head_norm(ka_ref[0].astype(F32), gka_ref[...], m128)
    oka_ref[0] = _rope128(zk, cos, sin, even16).astype(BF16)


def _qk_prep(proj, cos, sin, p):
    B, T, _ = proj.shape
    tm = min(T, 256)

    def col(name, w):
        blk = COLBLK[name]
        return pl.BlockSpec((1, tm, w), lambda b, i: (b, i, blk))

    vec = pl.BlockSpec((1, LANES), lambda b, i: (0, 0))
    tab = pl.BlockSpec((tm, LANES), lambda b, i: (i, 0))
    out = lambda w: pl.BlockSpec((1, tm, w), lambda b, i: (b, i, 0))
    return pl.pallas_call(
        _prep_kernel,
        grid=(B, T // tm),
        in_specs=[col("qA", 512), col("qB", 512), col("kB", 512), col("kA", 128), tab, tab,
                  vec, vec, vec, vec, pl.BlockSpec((LANES, LANES), lambda b, i: (0, 0))],
        out_specs=[out(512), out(128), out(512), out(512)],
        out_shape=[jax.ShapeDtypeStruct((B, T, 512), BF16), jax.ShapeDtypeStruct((B, T, 128), BF16),
                   jax.ShapeDtypeStruct((B, T, 512), BF16), jax.ShapeDtypeStruct((B, T, 512), BF16)],
        compiler_params=_cparams(("arbitrary", "arbitrary"), 32),
        name="qk_prep",
    )(proj, proj, proj, proj, cos, sin, p["qn_a2"], p["kn_a2"], p["qn_b2"], p["kn_b2"], p["m128"])


def _softmax_pv(parts, sink):
    m = parts[0][0].max(axis=-1, keepdims=True)
    for s, _ in parts[1:]:
        m = jnp.maximum(m, s.max(axis=-1, keepdims=True))
    if sink is not None:
        m = jnp.maximum(m, sink)
    den = None
    acc = None
    for s, v in parts:
        e = jnp.exp(s - m)
        ds = e.sum(axis=-1, keepdims=True)
        den = ds if den is None else den + ds
        pv = _dot(e.astype(BF16), v)
        acc = pv if acc is None else acc + pv
    if sink is not None:
        den = den + jnp.exp(sink - m)
    return acc / den


def _attn_a_kernel(sink_ref, q_ref, kp_ref, ko_ref, kn_ref, vp_ref, vo_ref, vn_ref, ck_ref, cv_ref, o_ref):
    n = pl.program_id(1)
    nb = pl.num_programs(1)
    q = q_ref[0]
    k_loc = jnp.concatenate([kp_ref[0], ko_ref[0], kn_ref[0]], axis=0)
    v_loc = jnp.concatenate([vp_ref[0], vo_ref[0], vn_ref[0]], axis=0).astype(BF16)
    ck = ck_ref[0]
    cv = cv_ref[0].astype(BF16)
    qi = lax.broadcasted_iota(jnp.int32, (A_BLOCK, 3 * A_BLOCK), 0)
    kc = lax.broadcasted_iota(jnp.int32, (A_BLOCK, 3 * A_BLOCK), 1)
    rel = kc - qi
    valid = (rel >= 0) & (rel <= 2 * A_BLOCK)
    valid = valid & ((n > 0) | (kc >= A_BLOCK)) & ((n < nb - 1) | (kc < 2 * A_BLOCK))
    rep = A_HEADS // A_KV_HEADS
    for h in range(A_HEADS):
        g = h // rep
        qh = q[:, h * HEAD_DIM:(h + 1) * HEAD_DIM]
        gs = slice(g * HEAD_DIM, (g + 1) * HEAD_DIM)
        s_loc = jnp.where(valid, _dot_nt(qh, k_loc[:, gs]), NEG)
        s_ctx = _dot_nt(qh, ck[:, gs])
        o = _softmax_pv([(s_loc, v_loc[:, gs]), (s_ctx, cv[:, gs])], sink_ref[h])
        o_ref[0, :, h * HEAD_DIM:(h + 1) * HEAD_DIM] = o.astype(o_ref.dtype)


def _attn_a(qa, ka, proj, cka, cproj, sink):
    B, S, _ = qa.shape
    L = cka.shape[1]
    nb = S // A_BLOCK
    vblk = COLBLK["vA"]
    blk = lambda f: pl.BlockSpec((1, A_BLOCK, 128), f)
    return pl.pallas_call(
        _attn_a_kernel,
        grid=(B, nb),
        in_specs=[pl.BlockSpec(memory_space=pltpu.SMEM),
                  pl.BlockSpec((1, A_BLOCK, 512), lambda b, n: (b, n, 0)),
                  blk(lambda b, n: (b, jnp.maximum(n - 1, 0), 0)),
                  blk(lambda b, n: (b, n, 0)),
                  blk(lambda b, n: (b, jnp.minimum(n + 1, nb - 1), 0)),
                  blk(lambda b, n: (b, jnp.maximum(n - 1, 0), vblk)),
                  blk(lambda b, n: (b, n, vblk)),
                  blk(lambda b, n: (b, jnp.minimum(n + 1, nb - 1), vblk)),
                  pl.BlockSpec((1, L, 128), lambda b, n: (b, 0, 0)),
                  pl.BlockSpec((1, L, 128), lambda b, n: (b, 0, vblk))],
        out_specs=pl.BlockSpec((1, A_BLOCK, 512), lambda b, n: (b, n, 0)),
        out_shape=jax.ShapeDtypeStruct((B, S, 512), BF16),
        compiler_params=_cparams(("arbitrary", "arbitrary"), 32),
        name="attn_window",
    )(sink, qa, ka, ka, ka, proj, proj, proj, cka, cproj)


def _attn_b_kernel(q_ref, k0_ref, k1_ref, k2_ref, v0_ref, v1_ref, v2_ref, ck_ref, cv_ref, bias_ref, o_ref):
    q = q_ref[0]
    k_loc = jnp.concatenate([k0_ref[0], k1_ref[0], k2_ref[0]], axis=0)
    v_loc = jnp.concatenate([v0_ref[0], v1_ref[0], v2_ref[0]], axis=0).astype(BF16)
    ck = ck_ref[0]
    cv = cv_ref[0].astype(BF16)
    for h in range(B_HEADS):
        hs = slice(h * HEAD_DIM, (h + 1) * HEAD_DIM)
        qh = q[:, hs]
        s_loc = _dot_nt(qh, k_loc[:, hs]) + bias_ref[0, h]
        s_ctx = _dot_nt(qh, ck[:, hs])
        o = _softmax_pv([(s_loc, v_loc[:, hs]), (s_ctx, cv[:, hs])], None)
        o_ref[0, :, hs] = o.astype(o_ref.dtype)


def _na_bias(rpb, rows):
    c = np.arange(GRID_W)
    col_start = np.clip(c - NA_COLS // 2, 0, GRID_W - NA_COLS)
    col_valid = (c[None, :] >= col_start[:, None]) & (c[None, :] < col_start[:, None] + NA_COLS)
    dc = np.clip(c[None, :] - c[:, None], -(NA_COLS - 1), NA_COLS - 1) + (NA_COLS - 1)
    toe = jnp.where(jnp.asarray(col_valid)[None, None], jnp.take(rpb, jnp.asarray(dc.reshape(-1)), axis=2)
                    .reshape(B_HEADS, 2 * NA_ROWS - 1, GRID_W, GRID_W), NEG)
    masked = jnp.full((B_HEADS, GRID_W, GRID_W), NEG, F32)
    tiles = []
    for r0 in (0, NA_TR, rows - NA_TR):
        band = []
        for rq in range(r0, r0 + NA_TR):
            start = min(max(rq - NA_ROWS // 2, 0), rows - NA_ROWS)
            blocks = [toe[:, kr - rq + NA_ROWS - 1] if start <= kr < start + NA_ROWS else masked
                      for kr in range(r0 - NA_TR, r0 + 2 * NA_TR)]
            band.append(jnp.concatenate(blocks, axis=2))
        tiles.append(jnp.concatenate(band, axis=1))
    return jnp.stack(tiles).astype(F32)


def _attn_b(qb, kb, proj, ckb, cproj, bias):
    B, S, _ = qb.shape
    L = ckb.shape[1]
    tq = NA_TR * GRID_W
    ns = S // tq
    vblk = COLBLK["vB"]
    blk = lambda f: pl.BlockSpec((1, tq, 512), f)
    lo = lambda b, i: (b, jnp.maximum(i - 1, 0), 0)
    hi = lambda b, i: (b, jnp.minimum(i + 1, ns - 1), 0)
    variant = lambda b, i: (jnp.where(i == 0, 0, jnp.where(i == ns - 1, 2, 1)), 0, 0, 0)
    return pl.pallas_call(
        _attn_b_kernel,
        grid=(B, ns),
        in_specs=[blk(lambda b, i: (b, i, 0)),
                  blk(lo), blk(lambda b, i: (b, i, 0)), blk(hi),
                  blk(lambda b, i: lo(b, i)[:2] + (vblk,)),
                  blk(lambda b, i: (b, i, vblk)),
                  blk(lambda b, i: hi(b, i)[:2] + (vblk,)),
                  pl.BlockSpec((1, L, 512), lambda b, i: (b, 0, 0)),
                  pl.BlockSpec((1, L, 512), lambda b, i: (b, 0, vblk)),
                  pl.BlockSpec((1, B_HEADS, tq, 3 * tq), variant)],
        out_specs=blk(lambda b, i: (b, i, 0)),
        out_shape=jax.ShapeDtypeStruct((B, S, 512), BF16),
        compiler_params=_cparams(("arbitrary", "arbitrary"), 56),
        name="attn_neighbourhood",
    )(qb, kb, kb, kb, proj, proj, proj, ckb, cproj, bias)


def _ctx_attn_kernel(sink_ref, q_ref, k_ref, v_ref, o_ref, *, kv_heads, use_sink):
    q = q_ref[0]
    k = k_ref[0]
    v = v_ref[0].astype(BF16)
    rep = A_HEADS // kv_heads
    for h in range(A_HEADS):
        g = h // rep
        gs = slice(g * HEAD_DIM, (g + 1) * HEAD_DIM)
        s = _dot_nt(q[:, h * HEAD_DIM:(h + 1) * HEAD_DIM], k[:, gs])
        o = _softmax_pv([(s, v[:, gs])], sink_ref[h] if use_sink else None)
        o_ref[0, :, h * HEAD_DIM:(h + 1) * HEAD_DIM] = o.astype(o_ref.dtype)


def _ctx_attn(q, k, cproj, vname, sink, kv_heads, use_sink):
    B, L, _ = q.shape
    kw = kv_heads * HEAD_DIM
    vblk = COLBLK[vname]
    return pl.pallas_call(
        functools.partial(_ctx_attn_kernel, kv_heads=kv_heads, use_sink=use_sink),
        grid=(B,),
        in_specs=[pl.BlockSpec(memory_space=pltpu.SMEM),
                  pl.BlockSpec((1, L, 512), lambda b: (b, 0, 0)),
                  pl.BlockSpec((1, L, kw), lambda b: (b, 0, 0)),
                  pl.BlockSpec((1, L, kw), lambda b: (b, 0, vblk))],
        out_specs=pl.BlockSpec((1, L, 512), lambda b: (b, 0, 0)),
        out_shape=jax.ShapeDtypeStruct((B, L, 512), BF16),
        compiler_params=_cparams(("arbitrary",), 32),
        name="attn_context",
    )(sink, q, k, cproj)


def _gla_direction(q_ref, k_ref, v_ref, a_ref, w2_ref, b_ref, o_ref, S, forward):
    C = q_ref.shape[1]
    lo = 0 if forward else C_GATE_RANK
    a = a_ref[0][:, lo:lo + C_GATE_RANK].astype(F32)
    z = _dot_precise(a, w2_ref[...]) + b_ref[...]
    la = (jnp.minimum(z, 0.0) - jnp.log1p(jnp.exp(-jnp.abs(z)))) * (1.0 / C_GATE_TAU)
    ri = lax.broadcasted_iota(jnp.int32, (C, C), 0)
    ci = lax.broadcasted_iota(jnp.int32, (C, C), 1)
    keep = (ci <= ri) if forward else (ci >= ri)
    tri = keep.astype(BF16)
    la_hi, la_lo = _split_bf16(la)
    cum = _dot(tri, la_hi) + _dot(tri, la_lo)
    ref_row = C // 2 - 1 if forward else C // 2
    end_row = C - 1 if forward else 0
    scale = C_DK ** -0.5
    for h in range(C_HEADS):
        ks = slice(h * C_DK, (h + 1) * C_DK)
        vs = slice(h * C_DV, (h + 1) * C_DV)
        c = cum[:, ks]
        c_ref = c[ref_row:ref_row + 1]
        c_end = c[end_row:end_row + 1]
        q = q_ref[0][:, ks].astype(F32) * scale
        k = k_ref[0][:, ks].astype(F32)
        v = v_ref[0][:, vs].astype(BF16)
        att = _dot_nt((q * jnp.exp(c - c_ref)).astype(BF16), (k * jnp.exp(c_ref - c)).astype(BF16))
        att = jnp.where(keep, att, 0.0)
        st = S[h]
        o = _dot(att.astype(BF16), v) + _dot_nt((q * jnp.exp(c)).astype(BF16), st.astype(BF16))
        kd = (k * jnp.exp(c_end - c)).astype(BF16)
        S[h] = st * jnp.exp(c_end) + _dot_tn(v, kd)
        o_ref[0, :, vs] = o


def _gla_kernel(qf, kf, vf, af, qb, kb, vb, ab, w2f, w2b, bf, bb, s0f, s0b, of, ob, sff, sfb, Sf, Sb):
    j = pl.program_id(1)

    @pl.when(j == 0)
    def _():
        Sf[...] = s0f[0]
        Sb[...] = s0b[0]

    _gla_direction(qf, kf, vf, af, w2f, bf, of, Sf, True)
    _gla_direction(qb, kb, vb, ab, w2b, bb, ob, Sb, False)

    @pl.when(j == pl.num_programs(1) - 1)
    def _():
        sff[0] = Sf[...]
        sfb[0] = Sb[...]


def _gla(proj, p, s0f, s0b):
    B, T, _ = proj.shape
    C = C_CHUNK
    n = T // C
    fwd = lambda b, j: (b, j)
    bwd = lambda b, j: (b, n - 1 - j)

    def cols(order):
        mk = lambda name, w: pl.BlockSpec((1, C, w), lambda b, j: order(b, j) + (COLBLK[name],))
        return [mk("qC", 512), mk("kC", 512), mk("vC", 1024), mk("aC", 128)]

    wspec = pl.BlockSpec((C_GATE_RANK, 512), lambda b, j: (0, 0))
    bspec = pl.BlockSpec((1, 512), lambda b, j: (0, 0))
    sspec = pl.BlockSpec((1, C_HEADS, C_DV, C_DK), lambda b, j: (b, 0, 0, 0))
    st = jax.ShapeDtypeStruct((B, C_HEADS, C_DV, C_DK), F32)
    return pl.pallas_call(
        _gla_kernel,
        grid=(B, n),
        in_specs=cols(fwd) + cols(bwd) + [wspec, wspec, bspec, bspec, sspec, sspec],
        out_specs=[pl.BlockSpec((1, C, 1024), lambda b, j: (b, j, 0)),
                   pl.BlockSpec((1, C, 1024), lambda b, j: (b, n - 1 - j, 0)),
                   sspec, sspec],
        out_shape=[jax.ShapeDtypeStruct((B, T, 1024), F32), jax.ShapeDtypeStruct((B, T, 1024), F32), st, st],
        scratch_shapes=[pltpu.VMEM((C_HEADS, C_DV, C_DK), F32), pltpu.VMEM((C_HEADS, C_DV, C_DK), F32)],
        compiler_params=_cparams(("arbitrary", "arbitrary"), 32),
        name="gla_scan",
    )(proj, proj, proj, proj, proj, proj, proj, proj, p["wa2_f"], p["wa2_b"], p["ba_f"], p["ba_b"], s0f, s0b)


def _merge_kernel(oa_ref, ob_ref, of_ref, obk_ref, r_ref, ga_ref, gb_ref, gc_ref, x_ref, g1_ref, sh_ref,
                  sc_ref, n2_ref, gn_ref, woa_ref, wob_ref, woc_ref, wout_ref, xo_ref, h2_ref):
    o = of_ref[0] + obk_ref[0]
    r = r_ref[0].astype(F32)
    parts = []
    for h in range(C_HEADS):
        vs = slice(h * C_DV, (h + 1) * C_DV)
        oh = o[:, vs]
        ms = jnp.mean(oh * oh, axis=-1, keepdims=True)
        rh = r[:, vs]
        parts.append((oh * lax.rsqrt(ms + EPS) * gn_ref[...]) * (rh * jax.nn.sigmoid(rh)))
    oc = jnp.concatenate(parts, axis=-1).astype(BF16)
    y = (jax.nn.sigmoid(ga_ref[0].astype(F32)) * _dot(oa_ref[0], woa_ref[...])
         + jax.nn.sigmoid(gb_ref[0].astype(F32)) * _dot(ob_ref[0], wob_ref[...])
         + jax.nn.sigmoid(gc_ref[0].astype(F32)) * _dot(oc, woc_ref[...]))
    xn = x_ref[0] + g1_ref[0] * _dot(y.astype(BF16), wout_ref[...])
    xo_ref[0] = xn
    h2_ref[0] = _modulated_norm(xn, n2_ref[...], sh_ref[0], sc_ref[0])


def _merge(oa, ob, of, obk, proj, x, g1, sh2, sc2, p):
    B, T, D = x.shape
    tm = min(T, 256)
    tok = lambda w: pl.BlockSpec((1, tm, w), lambda b, i: (b, i, 0))
    col = lambda name: pl.BlockSpec((1, tm, 1024), lambda b, i: (b, i, COLBLK[name]))
    mod = pl.BlockSpec((1, 1, D), lambda b, i: (b, 0, 0))
    full = lambda r, c: pl.BlockSpec((r, c), lambda b, i: (0, 0))
    return pl.pallas_call(
        _merge_kernel,
        grid=(B, T // tm),
        in_specs=[tok(512), tok(512), tok(1024), tok(1024), col("rC"), col("gA"), col("gB"), col("gC"),
                  tok(D), mod, mod, mod, full(1, D), full(1, C_DV),
                  full(512, D), full(512, D), full(1024, D), full(D, D)],
        out_specs=[tok(D), tok(D)],
        out_shape=[jax.ShapeDtypeStruct((B, T, D), F32), jax.ShapeDtypeStruct((B, T, D), F32)],
        compiler_params=_cparams(("arbitrary", "arbitrary"), 56),
        name="merge",
    )(oa, ob, of, obk, proj, proj, proj, proj, x, g1, sh2, sc2, p["norm2"], p["gn_c"],
      p["w_oa"], p["w_ob"], p["w_oc"], p["w_out"])


def _topk_rows(arr, rowid, k, payload=None):
    n = arr.shape[0]
    vals, ids = [], []
    for _ in range(k):
        m = arr.max(axis=0, keepdims=True)
        sel = jnp.where(arr == m, rowid, float(n)).min(axis=0, keepdims=True)
        hit = rowid == sel
        vals.append(m)
        ids.append(sel if payload is None else jnp.where(hit, payload, -1.0).max(axis=0, keepdims=True))
        arr = jnp.where(hit, -jnp.inf, arr)
    return jnp.concatenate(vals, axis=0), jnp.concatenate(ids, axis=0)


_PAIR_WIDTH = tuple(PEER_TOPK // (a + 1) for a in range(PEER_TOPK))


def _route_kernel(h_ref, wh_ref, wl_ref, g_ref, kh_ref, kl_ref, idx_ref, gate_ref):
    tm = h_ref.shape[0]
    hh, hl = _split_bf16(h_ref[...])
    q = _dot(hh, wh_ref[...]) + _dot(hh, wl_ref[...]) + _dot(hl, wh_ref[...])
    rowid = lax.broadcasted_iota(jnp.int32, (PEER_NKEYS, tm), 0).astype(F32)
    pos = lax.broadcasted_iota(jnp.int32, (sum(_PAIR_WIDTH), tm), 0).astype(F32)
    for p in range(PEER_HEADS):
        tops = []
        for half in range(2):
            seg = q[:, (2 * p + half) * PEER_HALF:(2 * p + half + 1) * PEER_HALF]
            ms = jnp.mean(seg * seg, axis=-1, keepdims=True)
            qh, ql = _split_bf16(seg * lax.rsqrt(ms + EPS) * g_ref[...])
            kh, kl = kh_ref[p, half], kl_ref[p, half]
            s_t = _dot_nt(kh, qh) + _dot_nt(kh, ql) + _dot_nt(kl, qh)
            tops.append(_topk_rows(s_t, rowid, PEER_TOPK))
        (s1, i1), (s2, i2) = tops
        cand = jnp.concatenate([s1[a:a + 1] + s2[:w] for a, w in enumerate(_PAIR_WIDTH)], axis=0)
        cidx = jnp.concatenate([i1[a:a + 1] * float(PEER_NKEYS) + i2[:w] for a, w in enumerate(_PAIR_WIDTH)],
                               axis=0)
        top, eid = _topk_rows(cand, pos, PEER_TOPK, payload=cidx)
        e = jnp.exp(top - top[0:1])
        rs = slice(p * PEER_TOPK, (p + 1) * PEER_TOPK)
        idx_ref[rs, :] = eid.astype(jnp.int32)
        gate_ref[rs, :] = e / e.sum(axis=0, keepdims=True)


def _peer_route(h2, p):
    T, D = h2.shape
    tm = min(T, 256)
    wspec = pl.BlockSpec((D, 2 * PEER_HEADS * PEER_HALF), lambda i: (0, 0))
    kspec = pl.BlockSpec((PEER_HEADS, 2, PEER_NKEYS, PEER_HALF), lambda i: (0, 0, 0, 0))
    return pl.pallas_call(
        _route_kernel,
        grid=(T // tm,),
        in_specs=[pl.BlockSpec((tm, D), lambda i: (i, 0)), wspec, wspec,
                  pl.BlockSpec((1, PEER_HALF), lambda i: (0, 0)), kspec, kspec],
        out_specs=[pl.BlockSpec((PEER_PICKS, tm), lambda i: (0, i)),
                   pl.BlockSpec((PEER_PICKS, tm), lambda i: (0, i))],
        out_shape=[jax.ShapeDtypeStruct((PEER_PICKS, T), jnp.int32),
                   jax.ShapeDtypeStruct((PEER_PICKS, T), F32)],
        compiler_params=_cparams(("arbitrary",), 48),
        name="peer_route",
    )(h2, p["w_pq_hi"], p["w_pq_lo"], p["pq_g"], p["sub_keys_hi"], p["sub_keys_lo"])


def _gather_kernel(idx_ref, gate_ref, h_ref, x_ref, g2_ref, uv_hbm, o_ref, buf, sem):
    tm = h_ref.shape[0]
    nbuf = buf.shape[0]
    ahead = nbuf - 1

    def issue(t, slot):
        for k in range(PEER_PICKS):
            row = pl.multiple_of(idx_ref[t * PEER_PICKS + k] * UV_CHUNKS, UV_CHUNKS)
            pltpu.make_async_copy(uv_hbm.at[pl.ds(row, UV_CHUNKS)],
                                  buf.at[slot, pl.ds(k * UV_CHUNKS, UV_CHUNKS)], sem.at[slot]).start()

    def wait(slot):
        pltpu.make_async_copy(uv_hbm.at[pl.ds(0, PEER_PICKS * UV_CHUNKS)], buf.at[slot], sem.at[slot]).wait()

    for t0 in range(ahead):
        issue(t0, t0)

    lane = lax.broadcasted_iota(jnp.int32, (PEER_PICKS, tm), 1)
    half = UV_CHUNKS // 2

    def token(t, slot, h_row):
        wait(slot)
        acc = None
        for c in range(half):
            u_c = buf[slot, pl.ds(c, PEER_PICKS, stride=UV_CHUNKS), :]
            term = u_c * h_row[:, c * LANES:(c + 1) * LANES]
            acc = term if acc is None else acc + term
        d = acc.sum(axis=1, keepdims=True)
        act = 0.5 * d * (1.0 + lax.erf(d * (2.0 ** -0.5)))
        gcol = jnp.where(lane == t, gate_ref[...], 0.0).sum(axis=1, keepdims=True)
        w = gcol * act
        mix = []
        for c in range(half):
            v_c = buf[slot, pl.ds(half + c, PEER_PICKS, stride=UV_CHUNKS), :]
            mix.append((v_c * w).sum(axis=0, keepdims=True))
        return jnp.concatenate(mix, axis=1)

    def group(g, carry):
        t0 = pl.multiple_of(g * nbuf, nbuf)
        h_rows = h_ref[pl.ds(t0, nbuf), :]
        outs = []
        for j in range(nbuf):
            nxt = t0 + j + ahead

            @pl.when(nxt < tm)
            def _():
                issue(nxt, (j + ahead) % nbuf)

            outs.append(token(t0 + j, j, h_rows[j:j + 1]))
        o_ref[pl.ds(t0, nbuf), :] = x_ref[pl.ds(t0, nbuf), :] + g2_ref[0] * jnp.concatenate(outs, axis=0)
        return carry

    lax.fori_loop(0, tm // nbuf, group, 0)


def _peer_gather(idx_flat, gate_t, h2, x, g2, uv, tokens_per_batch):
    T, D = h2.shape
    tm = GATHER_TM
    n = T // tm
    return pl.pallas_call(
        _gather_kernel,
        grid=(n,),
        in_specs=[pl.BlockSpec((tm * PEER_PICKS,), lambda i: (i,), memory_space=pltpu.SMEM),
                  pl.BlockSpec((PEER_PICKS, tm), lambda i: (0, i)),
                  pl.BlockSpec((tm, D), lambda i: (i, 0)),
                  pl.BlockSpec((tm, D), lambda i: (i, 0)),
                  pl.BlockSpec((1, 1, D), lambda i: ((i * tm) // tokens_per_batch, 0, 0)),
                  pl.BlockSpec(memory_space=pl.ANY)],
        out_specs=pl.BlockSpec((tm, D), lambda i: (i, 0)),
        out_shape=jax.ShapeDtypeStruct((T, D), F32),
        scratch_shapes=[pltpu.VMEM((GATHER_NBUF, PEER_PICKS * UV_CHUNKS, LANES), F32),
                        pltpu.SemaphoreType.DMA((GATHER_NBUF,))],
        compiler_params=_cparams(("arbitrary",), 32),
        name="peer_gather",
    )(idx_flat, gate_t, h2, x, g2, uv)


def _peer(h2, x, g2, p):
    B, T, D = x.shape
    hf = h2.reshape(B * T, D)
    idx_t, gate_t = _peer_route(hf, p)
    idx_flat = idx_t.T.reshape(-1)
    out = _peer_gather(idx_flat, gate_t, hf, x.reshape(B * T, D), g2, p["uv"], T)
    return out.reshape(B, T, D)


def _rearrange_w_in(w):
    cols = [w[:, o:o + n] for _, o, n, _ in _SEGS]
    cols.append(jnp.zeros((w.shape[0], PROJ_W - sum(n for _, _, n, _ in _SEGS)), w.dtype))
    return jnp.concatenate(cols, axis=1).astype(BF16)


def _rope_tables(S):
    t = jnp.arange(S)
    row = (t // GRID_W).astype(F32)[:, None]
    colp = (t % GRID_W).astype(F32)[:, None]
    nf = HEAD_DIM // 4
    inv = ROPE_THETA ** (-jnp.arange(nf, dtype=F32) / nf)
    ang = jnp.concatenate([row * inv, row * inv, colp * inv, colp * inv], axis=-1)
    ang = jnp.concatenate([ang, ang], axis=-1)
    return jnp.cos(ang), jnp.sin(ang)


def _layer_params(i, a):
    two = lambda g: jnp.concatenate([g, g]).reshape(1, LANES)
    hid = np.arange(LANES) // HEAD_DIM
    E = PEER_EXPERTS
    w_pq_hi, w_pq_lo = _split_bf16(a["w_pq"][i])
    keys_hi, keys_lo = _split_bf16(a["sub_keys"][i])
    return {
        "w_mod": a["w_mod"][i], "b_mod": a["b_mod"][i],
        "norm1": a["norm1"][i], "norm2": a["norm2"][i].reshape(1, D_MODEL),
        "w_in": _rearrange_w_in(a["w_in"][i]),
        "qn_a2": two(a["qn_a"][i]), "kn_a2": two(a["kn_a"][i]),
        "qn_b2": two(a["qn_b"][i]), "kn_b2": two(a["kn_b"][i]),
        "m128": jnp.asarray(hid[:, None] == hid[None, :], BF16),
        "sink_a": a["sink_a"][i], "rpb_b": a["rpb_b"][i],
        "wa2_f": a["wa2_f"][i], "wa2_b": a["wa2_b"][i],
        "ba_f": a["ba_f"][i].reshape(1, -1), "ba_b": a["ba_b"][i].reshape(1, -1),
        "gn_c": a["gn_c"][i].reshape(1, C_DV),
        "w_oa": a["w_oa"][i].astype(BF16), "w_ob": a["w_ob"][i].astype(BF16),
        "w_oc": a["w_oc"][i].astype(BF16), "w_out": a["w_out"][i].astype(BF16),
        "w_pq_hi": w_pq_hi, "w_pq_lo": w_pq_lo, "pq_g": a["pq_g"][i].reshape(1, PEER_HALF),
        "sub_keys_hi": keys_hi, "sub_keys_lo": keys_lo,
        "uv": jnp.concatenate([a["peer_u"][i].reshape(E, UV_CHUNKS // 2, LANES),
                               a["peer_v"][i].reshape(E, UV_CHUNKS // 2, LANES)],
                              axis=1).reshape(E * UV_CHUNKS, LANES),
    }


def _layer(x, ctx, c_all, p, cos, sin, last):
    B, S, D = x.shape
    L = ctx.shape[1]
    ones = jnp.ones((L, LANES), F32)
    zeros = jnp.zeros((L, LANES), F32)
    state0 = jnp.zeros((B, C_HEADS, C_DV, C_DK), F32)
    mods = _modulation(c_all, p["w_mod"], p["b_mod"])
    lat = [mods[:B, k * D:(k + 1) * D].reshape(B, 1, D) for k in range(N_MOD)]
    cm = [jnp.broadcast_to(mods[B, k * D:(k + 1) * D].reshape(1, 1, D), (B, 1, D)) for k in range(N_MOD)]
    sh1, sc1, g1, sh2, sc2, g2 = lat
    csh1, csc1, cg1, csh2, csc2, cg2 = cm

    proj = _norm_proj(x, p["norm1"], sh1, sc1, p["w_in"])
    cproj = _norm_proj(ctx, p["norm1"], csh1, csc1, p["w_in"])
    qa, ka, qb, kb = _qk_prep(proj, cos, sin, p)
    cqa, cka, cqb, ckb = _qk_prep(cproj, ones, zeros, p)

    o_a = _attn_a(qa, ka, proj, cka, cproj, p["sink_a"])
    o_b = _attn_b(qb, kb, proj, ckb, cproj, _na_bias(p["rpb_b"], S // GRID_W))
    co_f, co_b, s_f, s_b = _gla(cproj, p, state0, state0)
    o_f, o_bk, _, _ = _gla(proj, p, s_f, s_b)
    x, h2 = _merge(o_a, o_b, o_f, o_bk, proj, x, g1, sh2, sc2, p)
    x = _peer(h2, x, g2, p)
    if not last:
        co_a = _ctx_attn(cqa, cka, cproj, "vA", p["sink_a"], A_KV_HEADS, True)
        co_bb = _ctx_attn(cqb, ckb, cproj, "vB", p["sink_a"], B_HEADS, False)
        ctx, hc2 = _merge(co_a, co_bb, co_f, co_b, cproj, ctx, cg1, csh2, csc2, p)
        ctx = _peer(hc2, ctx, cg2, p)
    return x, ctx


def _forward(a):
    x, ctx = a["x"], a["ctx"]
    B, S, D = x.shape
    cos, sin = _rope_tables(S)
    c_all = jnp.zeros((8, D), F32).at[:B].set(a["c"]).at[B].set(a["c_ctx"])
    for i in range(DEPTH):
        x, ctx = _layer(x, ctx, c_all, _layer_params(i, a), cos, sin, i == DEPTH - 1)
    return x


def kernel(x, c, ctx, c_ctx, w_mod, b_mod, norm1, norm2, w_in, qn_a, kn_a, sink_a, qn_b, kn_b, rpb_b,
           wa2_f, ba_f, wa2_b, ba_b, gn_c, w_oa, w_ob, w_oc, w_out, w_pq, pq_g, sub_keys, peer_u, peer_v):
    return _forward(dict(
        x=x, c=c, ctx=ctx, c_ctx=c_ctx, w_mod=w_mod, b_mod=b_mod, norm1=norm1, norm2=norm2, w_in=w_in,
        qn_a=qn_a, kn_a=kn_a, sink_a=sink_a, qn_b=qn_b, kn_b=kn_b, rpb_b=rpb_b, wa2_f=wa2_f, ba_f=ba_f,
        wa2_b=wa2_b, ba_b=ba_b, gn_c=gn_c, w_oa=w_oa, w_ob=w_ob, w_oc=w_oc, w_out=w_out, w_pq=w_pq,
        pq_g=pq_g, sub_keys=sub_keys, peer_u=peer_u, peer_v=peer_v))
```

```python
import dataclasses
import functools

import numpy as np
import jax
import jax.numpy as jnp
from jax import lax
from jax.experimental import pallas as pl
from jax.experimental.pallas import tpu as pltpu
from jax.experimental.pallas import tpu_sc as plsc

F32 = jnp.float32
BF16 = jnp.bfloat16

D_MODEL = 1024
DEPTH = 4
GRID_W = 64
HEAD_DIM = 64
ROPE_THETA = 10000.0
EPS = 1e-6
NEG = -1e30
N_MOD = 6
A_HEADS = 8
A_KV_HEADS = 2
A_BLOCK = 128
B_HEADS = 8
NA_ROWS = 8
NA_COLS = 16
C_HEADS = 4
C_DK = 128
C_DV = 256
C_GATE_RANK = 16
C_GATE_TAU = 16.0
C_CHUNK = 64
PEER_HEADS = 8
PEER_NKEYS = 128
PEER_EXPERTS = PEER_NKEYS * PEER_NKEYS
PEER_HALF = 128
PEER_TOPK = 16
PEER_PICKS = PEER_HEADS * PEER_TOPK

LANES = 128
MIB = 1024 * 1024

_SEGS = (
    ("vC", 1792, 1024, 0), ("rC", 4384, 1024, 1024), ("gA", 5408, 1024, 2048),
    ("gB", 6432, 1024, 3072), ("gC", 7456, 1024, 4096), ("kB", 256, 512, 5120),
    ("vB", 768, 512, 5632), ("kC", 1280, 512, 6144), ("qA", 2848, 512, 6656),
    ("qB", 3360, 512, 7168), ("qC", 3872, 512, 7680), ("kA", 0, 128, 8192),
    ("vA", 128, 128, 8320), ("aC", 2816, 32, 8448),
)
PROJ_W = 8704
PROJ_TN = 2176
COLBLK = {name: new // (LANES if w < LANES else w) for name, _, w, new in _SEGS}
PROJ_DT = F32

NA_TR = 4
GATHER_TM = 128
GATHER_NBUF = 8
UV_CHUNKS = 16


def _cparams(sem, vmem_mib):
    return pltpu.CompilerParams(dimension_semantics=sem, vmem_limit_bytes=vmem_mib * MIB)


def _split_bf16(a):
    hi = a.astype(BF16)
    lo = (a - hi.astype(F32)).astype(BF16)
    return hi, lo


def _dot(a, b):
    return jnp.dot(a, b, preferred_element_type=F32)


def _dot_nt(a, b):
    return lax.dot_general(a, b, (((1,), (1,)), ((), ())), preferred_element_type=F32)


def _dot_tn(a, b):
    return lax.dot_general(a, b, (((0,), (0,)), ((), ())), preferred_element_type=F32)


def _dot_precise(a, b):
    ah, al = _split_bf16(a)
    bh, bl = _split_bf16(b)
    return _dot(ah, bh) + _dot(ah, bl) + _dot(al, bh)


def _mod_kernel(c_ref, w_ref, b_ref, o_ref):
    c = c_ref[...]
    act = c * jax.nn.sigmoid(c)
    o_ref[...] = _dot_precise(act, w_ref[...]) + b_ref[...]


def _modulation(c_all, w_mod, b_mod):
    n = w_mod.shape[1]
    tn = D_MODEL
    return pl.pallas_call(
        _mod_kernel,
        grid=(n // tn,),
        in_specs=[pl.BlockSpec((8, D_MODEL), lambda j: (0, 0)),
                  pl.BlockSpec((D_MODEL, tn), lambda j: (0, j)),
                  pl.BlockSpec((1, tn), lambda j: (0, j))],
        out_specs=pl.BlockSpec((8, tn), lambda j: (0, j)),
        out_shape=jax.ShapeDtypeStruct((8, n), F32),
        compiler_params=_cparams(("arbitrary",), 32),
        name="modulation",
    )(c_all, w_mod, b_mod.reshape(1, n))


def _modulated_norm(x, g, sh, sc):
    ms = jnp.mean(x * x, axis=-1, keepdims=True)
    return (x * lax.rsqrt(ms + EPS) * g) * (1.0 + sc) + sh


def _norm_proj_kernel(x_ref, g_ref, sh_ref, sc_ref, w_ref, o_ref):
    h = _modulated_norm(x_ref[0], g_ref[...], sh_ref[0], sc_ref[0])
    o_ref[0] = _dot(h.astype(BF16), w_ref[...]).astype(o_ref.dtype)


def _norm_proj(x, g, sh, sc, w):
    B, T, D = x.shape
    tm = min(T, 512)
    return pl.pallas_call(
        _norm_proj_kernel,
        grid=(PROJ_W // PROJ_TN, B, T // tm),
        in_specs=[pl.BlockSpec((1, tm, D), lambda j, b, i: (b, i, 0)),
                  pl.BlockSpec((1, D), lambda j, b, i: (0, 0)),
                  pl.BlockSpec((1, 1, D), lambda j, b, i: (b, 0, 0)),
                  pl.BlockSpec((1, 1, D), lambda j, b, i: (b, 0, 0)),
                  pl.BlockSpec((D, PROJ_TN), lambda j, b, i: (0, j))],
        out_specs=pl.BlockSpec((1, tm, PROJ_TN), lambda j, b, i: (b, i, j)),
        out_shape=jax.ShapeDtypeStruct((B, T, PROJ_W), PROJ_DT),
        compiler_params=_cparams(("arbitrary", "arbitrary", "arbitrary"), 48),
        name="norm_proj",
    )(x, g.reshape(1, D), sh, sc, w)


def _head_norm(z, g2, m128):
    hi, lo = _split_bf16(z * z)
    ss = _dot(hi, m128) + _dot(lo, m128)
    return z * lax.rsqrt(ss * (1.0 / HEAD_DIM) + EPS) * g2


def _rope128(z, cos, sin, even16):
    rot = jnp.where(even16, -pltpu.roll(z, LANES - 16, 1), pltpu.roll(z, 16, 1))
    return z * cos + rot * sin


def _prep_kernel(qa_ref, qb_ref, kb_ref, ka_ref, cos_ref, sin_ref, gqa_ref, gka_ref, gqb_ref,
                 gkb_ref, m_ref, oqa_ref, oka_ref, oqb_ref, okb_ref):
    m128 = m_ref[...]
    cos = cos_ref[...]
    sin = sin_ref[...]
    lane = lax.broadcasted_iota(jnp.int32, cos.shape, 1)
    even16 = ((lane // 16) % 2) == 0
    scale = HEAD_DIM ** -0.5
    for s in range(4):
        sl = slice(s * LANES, (s + 1) * LANES)
        za = _head_norm(qa_ref[0][:, sl].astype(F32), gqa_ref[...], m128)
        oqa_ref[0, :, sl] = (_rope128(za, cos, sin, even16) * scale).astype(BF16)
        zb = _head_norm(qb_ref[0][:, sl].astype(F32), gqb_ref[...], m128)
        oqb_ref[0, :, sl] = (zb * scale).astype(BF16)
        zk = _head_norm(kb_ref[0][:, sl].astype(F32), gkb_ref[...], m128)
        okb_ref[0, :, sl] = zk.astype(BF16)
    zk = _head_norm(ka_ref[0].astype(F32), gka_ref[...], m128)
    oka_ref[0] = _rope128(zk, cos, sin, even16).astype(BF16)


def _qk_prep(proj, cos, sin, p):
    B, T, _ = proj.shape
    tm = min(T, 256)

    def col(name, w):
        blk = COLBLK[name]
        return pl.BlockSpec((1, tm, w), lambda b, i: (b, i, blk))

    vec = pl.BlockSpec((1, LANES), lambda b, i: (0, 0))
    tab = pl.BlockSpec((tm, LANES), lambda b, i: (i, 0))
    out = lambda w: pl.BlockSpec((1, tm, w), lambda b, i: (b, i, 0))
    return pl.pallas_call(
        _prep_kernel,
        grid=(B, T // tm),
        in_specs=[col("qA", 512), col("qB", 512), col("kB", 512), col("kA", 128), tab, tab,
                  vec, vec, vec, vec, pl.BlockSpec((LANES, LANES), lambda b, i: (0, 0))],
        out_specs=[out(512), out(128), out(512), out(512)],
        out_shape=[jax.ShapeDtypeStruct((B, T, 512), BF16), jax.ShapeDtypeStruct((B, T, 128), BF16),
                   jax.ShapeDtypeStruct((B, T, 512), BF16), jax.ShapeDtypeStruct((B, T, 512), BF16)],
        compiler_params=_cparams(("arbitrary", "arbitrary"), 32),
        name="qk_prep",
    )(proj, proj, proj, proj, cos, sin, p["qn_a2"], p["kn_a2"], p["qn_b2"], p["kn_b2"], p["m128"])


def _softmax_pv(parts, sink):
    m = parts[0][0].max(axis=-1, keepdims=True)
    for s, _ in parts[1:]:
        m = jnp.maximum(m, s.max(axis=-1, keepdims=True))
    if sink is not None:
        m = jnp.maximum(m, sink)
    den = None
    acc = None
    for s, v in parts:
        e = jnp.exp(s - m)
        ds = e.sum(axis=-1, keepdims=True)
        den = ds if den is None else den + ds
        pv = _dot(e.astype(BF16), v)
        acc = pv if acc is None else acc + pv
    if sink is not None:
        den = den + jnp.exp(sink - m)
    return acc / den


def _attn_a_kernel(sink_ref, q_ref, kp_ref, ko_ref, kn_ref, vp_ref, vo_ref, vn_ref, ck_ref, cv_ref, o_ref):
    n = pl.program_id(1)
    nb = pl.num_programs(1)
    q = q_ref[0]
    k_loc = jnp.concatenate([kp_ref[0], ko_ref[0], kn_ref[0]], axis=0)
    v_loc = jnp.concatenate([vp_ref[0], vo_ref[0], vn_ref[0]], axis=0).astype(BF16)
    ck = ck_ref[0]
    cv = cv_ref[0].astype(BF16)
    qi = lax.broadcasted_iota(jnp.int32, (A_BLOCK, 3 * A_BLOCK), 0)
    kc = lax.broadcasted_iota(jnp.int32, (A_BLOCK, 3 * A_BLOCK), 1)
    rel = kc - qi
    valid = (rel >= 0) & (rel <= 2 * A_BLOCK)
    valid = valid & ((n > 0) | (kc >= A_BLOCK)) & ((n < nb - 1) | (kc < 2 * A_BLOCK))
    rep = A_HEADS // A_KV_HEADS
    for h in range(A_HEADS):
        g = h // rep
        qh = q[:, h * HEAD_DIM:(h + 1) * HEAD_DIM]
        gs = slice(g * HEAD_DIM, (g + 1) * HEAD_DIM)
        s_loc = jnp.where(valid, _dot_nt(qh, k_loc[:, gs]), NEG)
        s_ctx = _dot_nt(qh, ck[:, gs])
        o = _softmax_pv([(s_loc, v_loc[:, gs]), (s_ctx, cv[:, gs])], sink_ref[h])
        o_ref[0, :, h * HEAD_DIM:(h + 1) * HEAD_DIM] = o.astype(o_ref.dtype)


def _attn_a(qa, ka, proj, cka, cproj, sink):
    B, S, _ = qa.shape
    L = cka.shape[1]
    nb = S // A_BLOCK
    vblk = COLBLK["vA"]
    blk = lambda f: pl.BlockSpec((1, A_BLOCK, 128), f)
    return pl.pallas_call(
        _attn_a_kernel,
        grid=(B, nb),
        in_specs=[pl.BlockSpec(memory_space=pltpu.SMEM),
                  pl.BlockSpec((1, A_BLOCK, 512), lambda b, n: (b, n, 0)),
                  blk(lambda b, n: (b, jnp.maximum(n - 1, 0), 0)),
                  blk(lambda b, n: (b, n, 0)),
                  blk(lambda b, n: (b, jnp.minimum(n + 1, nb - 1), 0)),
                  blk(lambda b, n: (b, jnp.maximum(n - 1, 0), vblk)),
                  blk(lambda b, n: (b, n, vblk)),
                  blk(lambda b, n: (b, jnp.minimum(n + 1, nb - 1), vblk)),
                  pl.BlockSpec((1, L, 128), lambda b, n: (b, 0, 0)),
                  pl.BlockSpec((1, L, 128), lambda b, n: (b, 0, vblk))],
        out_specs=pl.BlockSpec((1, A_BLOCK, 512), lambda b, n: (b, n, 0)),
        out_shape=jax.ShapeDtypeStruct((B, S, 512), BF16),
        compiler_params=_cparams(("arbitrary", "arbitrary"), 32),
        name="attn_window",
    )(sink, qa, ka, ka, ka, proj, proj, proj, cka, cproj)


def _attn_b_kernel(q_ref, k0_ref, k1_ref, k2_ref, v0_ref, v1_ref, v2_ref, ck_ref, cv_ref, bias_ref, o_ref):
    q = q_ref[0]
    k_loc = jnp.concatenate([k0_ref[0], k1_ref[0], k2_ref[0]], axis=0)
    v_loc = jnp.concatenate([v0_ref[0], v1_ref[0], v2_ref[0]], axis=0).astype(BF16)
    ck = ck_ref[0]
    cv = cv_ref[0].astype(BF16)
    for h in range(B_HEADS):
        hs = slice(h * HEAD_DIM, (h + 1) * HEAD_DIM)
        qh = q[:, hs]
        s_loc = _dot_nt(qh, k_loc[:, hs]) + bias_ref[0, h]
        s_ctx = _dot_nt(qh, ck[:, hs])
        o = _softmax_pv([(s_loc, v_loc[:, hs]), (s_ctx, cv[:, hs])], None)
        o_ref[0, :, hs] = o.astype(o_ref.dtype)


def _na_bias(rpb, rows):
    c = np.arange(GRID_W)
    col_start = np.clip(c - NA_COLS // 2, 0, GRID_W - NA_COLS)
    col_valid = (c[None, :] >= col_start[:, None]) & (c[None, :] < col_start[:, None] + NA_COLS)
    dc = np.clip(c[None, :] - c[:, None], -(NA_COLS - 1), NA_COLS - 1) + (NA_COLS - 1)
    toe = jnp.where(jnp.asarray(col_valid)[None, None], jnp.take(rpb, jnp.asarray(dc.reshape(-1)), axis=2)
                    .reshape(B_HEADS, 2 * NA_ROWS - 1, GRID_W, GRID_W), NEG)
    masked = jnp.full((B_HEADS, GRID_W, GRID_W), NEG, F32)
    tiles = []
    for r0 in (0, NA_TR, rows - NA_TR):
        band = []
        for rq in range(r0, r0 + NA_TR):
            start = min(max(rq - NA_ROWS // 2, 0), rows - NA_ROWS)
            blocks = [toe[:, kr - rq + NA_ROWS - 1] if start <= kr < start + NA_ROWS else masked
                      for kr in range(r0 - NA_TR, r0 + 2 * NA_TR)]
            band.append(jnp.concatenate(blocks, axis=2))
        tiles.append(jnp.concatenate(band, axis=1))
    return jnp.stack(tiles).astype(F32)


def _attn_b(qb, kb, proj, ckb, cproj, bias):
    B, S, _ = qb.shape
    L = ckb.shape[1]
    tq = NA_TR * GRID_W
    ns = S // tq
    vblk = COLBLK["vB"]
    blk = lambda f: pl.BlockSpec((1, tq, 512), f)
    lo = lambda b, i: (b, jnp.maximum(i - 1, 0), 0)
    hi = lambda b, i: (b, jnp.minimum(i + 1, ns - 1), 0)
    variant = lambda b, i: (jnp.where(i == 0, 0, jnp.where(i == ns - 1, 2, 1)), 0, 0, 0)
    return pl.pallas_call(
        _attn_b_kernel,
        grid=(B, ns),
        in_specs=[blk(lambda b, i: (b, i, 0)),
                  blk(lo), blk(lambda b, i: (b, i, 0)), blk(hi),
                  blk(lambda b, i: lo(b, i)[:2] + (vblk,)),
                  blk(lambda b, i: (b, i, vblk)),
                  blk(lambda b, i: hi(b, i)[:2] + (vblk,)),
                  pl.BlockSpec((1, L, 512), lambda b, i: (b, 0, 0)),
                  pl.BlockSpec((1, L, 512), lambda b, i: (b, 0, vblk)),
                  pl.BlockSpec((1, B_HEADS, tq, 3 * tq), variant)],
        out_specs=blk(lambda b, i: (b, i, 0)),
        out_shape=jax.ShapeDtypeStruct((B, S, 512), BF16),
        compiler_params=_cparams(("arbitrary", "arbitrary"), 56),
        name="attn_neighbourhood",
    )(qb, kb, kb, kb, proj, proj, proj, ckb, cproj, bias)


def _ctx_attn_kernel(sink_ref, q_ref, k_ref, v_ref, o_ref, *, kv_heads, use_sink):
    q = q_ref[0]
    k = k_ref[0]
    v = v_ref[0].astype(BF16)
    rep = A_HEADS // kv_heads
    for h in range(A_HEADS):
        g = h // rep
        gs = slice(g * HEAD_DIM, (g + 1) * HEAD_DIM)
        s = _dot_nt(q[:, h * HEAD_DIM:(h + 1) * HEAD_DIM], k[:, gs])
        o = _softmax_pv([(s, v[:, gs])], sink_ref[h] if use_sink else None)
        o_ref[0, :, h * HEAD_DIM:(h + 1) * HEAD_DIM] = o.astype(o_ref.dtype)


def _ctx_attn(q, k, cproj, vname, sink, kv_heads, use_sink):
    B, L, _ = q.shape
    kw = kv_heads * HEAD_DIM
    vblk = COLBLK[vname]
    return pl.pallas_call(
        functools.partial(_ctx_attn_kernel, kv_heads=kv_heads, use_sink=use_sink),
        grid=(B,),
        in_specs=[pl.BlockSpec(memory_space=pltpu.SMEM),
                  pl.BlockSpec((1, L, 512), lambda b: (b, 0, 0)),
                  pl.BlockSpec((1, L, kw), lambda b: (b, 0, 0)),
                  pl.BlockSpec((1, L, kw), lambda b: (b, 0, vblk))],
        out_specs=pl.BlockSpec((1, L, 512), lambda b: (b, 0, 0)),
        out_shape=jax.ShapeDtypeStruct((B, L, 512), BF16),
        compiler_params=_cparams(("arbitrary",), 32),
        name="attn_context",
    )(sink, q, k, cproj)


def _gla_direction(q_ref, k_ref, v_ref, a_ref, w2_ref, b_ref, o_ref, S, forward):
    C = q_ref.shape[1]
    lo = 0 if forward else C_GATE_RANK
    a = a_ref[0][:, lo:lo + C_GATE_RANK].astype(F32)
    z = _dot_precise(a, w2_ref[...]) + b_ref[...]
    la = (jnp.minimum(z, 0.0) - jnp.log1p(jnp.exp(-jnp.abs(z)))) * (1.0 / C_GATE_TAU)
    ri = lax.broadcasted_iota(jnp.int32, (C, C), 0)
    ci = lax.broadcasted_iota(jnp.int32, (C, C), 1)
    keep = (ci <= ri) if forward else (ci >= ri)
    tri = keep.astype(BF16)
    la_hi, la_lo = _split_bf16(la)
    cum = _dot(tri, la_hi) + _dot(tri, la_lo)
    ref_row = C // 2 - 1 if forward else C // 2
    end_row = C - 1 if forward else 0
    scale = C_DK ** -0.5
    for h in range(C_HEADS):
        ks = slice(h * C_DK, (h + 1) * C_DK)
        vs = slice(h * C_DV, (h + 1) * C_DV)
        c = cum[:, ks]
        c_ref = c[ref_row:ref_row + 1]
        c_end = c[end_row:end_row + 1]
        q = q_ref[0][:, ks].astype(F32) * scale
        k = k_ref[0][:, ks].astype(F32)
        v = v_ref[0][:, vs].astype(BF16)
        att = _dot_nt((q * jnp.exp(c - c_ref)).astype(BF16), (k * jnp.exp(c_ref - c)).astype(BF16))
        att = jnp.where(keep, att, 0.0)
        st = S[h]
        o = _dot(att.astype(BF16), v) + _dot_nt((q * jnp.exp(c)).astype(BF16), st.astype(BF16))
        kd = (k * jnp.exp(c_end - c)).astype(BF16)
        S[h] = st * jnp.exp(c_end) + _dot_tn(v, kd)
        o_ref[0, :, vs] = o


def _gla_kernel(qf, kf, vf, af, qb, kb, vb, ab, w2f, w2b, bf, bb, s0f, s0b, of, ob, sff, sfb, Sf, Sb):
    j = pl.program_id(1)

    @pl.when(j == 0)
    def _():
        Sf[...] = s0f[0]
        Sb[...] = s0b[0]

    _gla_direction(qf, kf, vf, af, w2f, bf, of, Sf, True)
    _gla_direction(qb, kb, vb, ab, w2b, bb, ob, Sb, False)

    @pl.when(j == pl.num_programs(1) - 1)
    def _():
        sff[0] = Sf[...]
        sfb[0] = Sb[...]


def _gla(proj, p, s0f, s0b):
    B, T, _ = proj.shape
    C = C_CHUNK
    n = T // C
    fwd = lambda b, j: (b, j)
    bwd = lambda b, j: (b, n - 1 - j)

    def cols(order):
        mk = lambda name, w: pl.BlockSpec((1, C, w), lambda b, j: order(b, j) + (COLBLK[name],))
        return [mk("qC", 512), mk("kC", 512), mk("vC", 1024), mk("aC", 128)]

    wspec = pl.BlockSpec((C_GATE_RANK, 512), lambda b, j: (0, 0))
    bspec = pl.BlockSpec((1, 512), lambda b, j: (0, 0))
    sspec = pl.BlockSpec((1, C_HEADS, C_DV, C_DK), lambda b, j: (b, 0, 0, 0))
    st = jax.ShapeDtypeStruct((B, C_HEADS, C_DV, C_DK), F32)
    return pl.pallas_call(
        _gla_kernel,
        grid=(B, n),
        in_specs=cols(fwd) + cols(bwd) + [wspec, wspec, bspec, bspec, sspec, sspec],
        out_specs=[pl.BlockSpec((1, C, 1024), lambda b, j: (b, j, 0)),
                   pl.BlockSpec((1, C, 1024), lambda b, j: (b, n - 1 - j, 0)),
                   sspec, sspec],
        out_shape=[jax.ShapeDtypeStruct((B, T, 1024), F32), jax.ShapeDtypeStruct((B, T, 1024), F32), st, st],
        scratch_shapes=[pltpu.VMEM((C_HEADS, C_DV, C_DK), F32), pltpu.VMEM((C_HEADS, C_DV, C_DK), F32)],
        compiler_params=_cparams(("arbitrary", "arbitrary"), 32),
        name="gla_scan",
    )(proj, proj, proj, proj, proj, proj, proj, proj, p["wa2_f"], p["wa2_b"], p["ba_f"], p["ba_b"], s0f, s0b)


def _merge_kernel(oa_ref, ob_ref, of_ref, obk_ref, r_ref, ga_ref, gb_ref, gc_ref, x_ref, g1_ref, sh_ref,
                  sc_ref, n2_ref, gn_ref, woa_ref, wob_ref, woc_ref, wout_ref, xo_ref, h2_ref):
    o = of_ref[0] + obk_ref[0]
    r = r_ref[0].astype(F32)
    parts = []
    for h in range(C_HEADS):
        vs = slice(h * C_DV, (h + 1) * C_DV)
        oh = o[:, vs]
        ms = jnp.mean(oh * oh, axis=-1, keepdims=True)
        rh = r[:, vs]
        parts.append((oh * lax.rsqrt(ms + EPS) * gn_ref[...]) * (rh * jax.nn.sigmoid(rh)))
    oc = jnp.concatenate(parts, axis=-1).astype(BF16)
    y = (jax.nn.sigmoid(ga_ref[0].astype(F32)) * _dot(oa_ref[0], woa_ref[...])
         + jax.nn.sigmoid(gb_ref[0].astype(F32)) * _dot(ob_ref[0], wob_ref[...])
         + jax.nn.sigmoid(gc_ref[0].astype(F32)) * _dot(oc, woc_ref[...]))
    xn = x_ref[0] + g1_ref[0] * _dot(y.astype(BF16), wout_ref[...])
    xo_ref[0] = xn
    h2_ref[0] = _modulated_norm(xn, n2_ref[...], sh_ref[0], sc_ref[0])


def _merge(oa, ob, of, obk, proj, x, g1, sh2, sc2, p):
    B, T, D = x.shape
    tm = min(T, 256)
    tok = lambda w: pl.BlockSpec((1, tm, w), lambda b, i: (b, i, 0))
    col = lambda name: pl.BlockSpec((1, tm, 1024), lambda b, i: (b, i, COLBLK[name]))
    mod = pl.BlockSpec((1, 1, D), lambda b, i: (b, 0, 0))
    full = lambda r, c: pl.BlockSpec((r, c), lambda b, i: (0, 0))
    return pl.pallas_call(
        _merge_kernel,
        grid=(B, T // tm),
        in_specs=[tok(512), tok(512), tok(1024), tok(1024), col("rC"), col("gA"), col("gB"), col("gC"),
                  tok(D), mod, mod, mod, full(1, D), full(1, C_DV),
                  full(512, D), full(512, D), full(1024, D), full(D, D)],
        out_specs=[tok(D), tok(D)],
        out_shape=[jax.ShapeDtypeStruct((B, T, D), F32), jax.ShapeDtypeStruct((B, T, D), F32)],
        compiler_params=_cparams(("arbitrary", "arbitrary"), 56),
        name="merge",
    )(oa, ob, of, obk, proj, proj, proj, proj, x, g1, sh2, sc2, p["norm2"], p["gn_c"],
      p["w_oa"], p["w_ob"], p["w_oc"], p["w_out"])


def _topk_rows(arr, rowid, k, payload=None):
    n = arr.shape[0]
    vals, ids = [], []
    for _ in range(k):
        m = arr.max(axis=0, keepdims=True)
        sel = jnp.where(arr == m, rowid, float(n)).min(axis=0, keepdims=True)
        hit = rowid == sel
        vals.append(m)
        ids.append(sel if payload is None else jnp.where(hit, payload, -1.0).max(axis=0, keepdims=True))
        arr = jnp.where(hit, -jnp.inf, arr)
    return jnp.concatenate(vals, axis=0), jnp.concatenate(ids, axis=0)


_PAIR_WIDTH = tuple(PEER_TOPK // (a + 1) for a in range(PEER_TOPK))


def _route_kernel(h_ref, wh_ref, wl_ref, g_ref, kh_ref, kl_ref, idx_ref, gate_ref):
    tm = h_ref.shape[0]
    hh, hl = _split_bf16(h_ref[...])
    q = _dot(hh, wh_ref[...]) + _dot(hh, wl_ref[...]) + _dot(hl, wh_ref[...])
    rowid = lax.broadcasted_iota(jnp.int32, (PEER_NKEYS, tm), 0).astype(F32)
    pos = lax.broadcasted_iota(jnp.int32, (sum(_PAIR_WIDTH), tm), 0).astype(F32)
    for p in range(PEER_HEADS):
        tops = []
        for half in range(2):
            seg = q[:, (2 * p + half) * PEER_HALF:(2 * p + half + 1) * PEER_HALF]
            ms = jnp.mean(seg * seg, axis=-1, keepdims=True)
            qh, ql = _split_bf16(seg * lax.rsqrt(ms + EPS) * g_ref[...])
            kh, kl = kh_ref[p, half], kl_ref[p, half]
            s_t = _dot_nt(kh, qh) + _dot_nt(kh, ql) + _dot_nt(kl, qh)
            tops.append(_topk_rows(s_t, rowid, PEER_TOPK))
        (s1, i1), (s2, i2) = tops
        cand = jnp.concatenate([s1[a:a + 1] + s2[:w] for a, w in enumerate(_PAIR_WIDTH)], axis=0)
        cidx = jnp.concatenate([i1[a:a + 1] * float(PEER_NKEYS) + i2[:w] for a, w in enumerate(_PAIR_WIDTH)],
                               axis=0)
        top, eid = _topk_rows(cand, pos, PEER_TOPK, payload=cidx)
        e = jnp.exp(top - top[0:1])
        rs = slice(p * PEER_TOPK, (p + 1) * PEER_TOPK)
        idx_ref[rs, :] = eid.astype(jnp.int32)
        gate_ref[rs, :] = e / e.sum(axis=0, keepdims=True)


def _peer_route(h2, p):
    T, D = h2.shape
    tm = min(T, 256)
    wspec = pl.BlockSpec((D, 2 * PEER_HEADS * PEER_HALF), lambda i: (0, 0))
    kspec = pl.BlockSpec((PEER_HEADS, 2, PEER_NKEYS, PEER_HALF), lambda i: (0, 0, 0, 0))
    return pl.pallas_call(
        _route_kernel,
        grid=(T // tm,),
        in_specs=[pl.BlockSpec((tm, D), lambda i: (i, 0)), wspec, wspec,
                  pl.BlockSpec((1, PEER_HALF), lambda i: (0, 0)), kspec, kspec],
        out_specs=[pl.BlockSpec((PEER_PICKS, tm), lambda i: (0, i)),
                   pl.BlockSpec((PEER_PICKS, tm), lambda i: (0, i))],
        out_shape=[jax.ShapeDtypeStruct((PEER_PICKS, T), jnp.int32),
                   jax.ShapeDtypeStruct((PEER_PICKS, T), F32)],
        compiler_params=_cparams(("arbitrary",), 48),
        name="peer_route",
    )(h2, p["w_pq_hi"], p["w_pq_lo"], p["pq_g"], p["sub_keys_hi"], p["sub_keys_lo"])


def _gather_kernel(idx_ref, gate_ref, h_ref, x_ref, g2_ref, uv_hbm, dep_hbm, o_ref, buf, sem):
    tm = h_ref.shape[0]
    nbuf = buf.shape[0]
    ahead = nbuf - 1

    def issue(t, slot):
        for k in range(PEER_PICKS):
            row = pl.multiple_of(idx_ref[t * PEER_PICKS + k] * UV_CHUNKS, UV_CHUNKS)
            pltpu.make_async_copy(uv_hbm.at[pl.ds(row, UV_CHUNKS)],
                                  buf.at[slot, pl.ds(k * UV_CHUNKS, UV_CHUNKS)], sem.at[slot]).start()

    def wait(slot):
        pltpu.make_async_copy(uv_hbm.at[pl.ds(0, PEER_PICKS * UV_CHUNKS)], buf.at[slot], sem.at[slot]).wait()

    for t0 in range(ahead):
        issue(t0, t0)

    lane = lax.broadcasted_iota(jnp.int32, (PEER_PICKS, tm), 1)
    half = UV_CHUNKS // 2

    def token(t, slot, h_row):
        wait(slot)
        acc = None
        for c in range(half):
            u_c = buf[slot, pl.ds(c, PEER_PICKS, stride=UV_CHUNKS), :]
            term = u_c * h_row[:, c * LANES:(c + 1) * LANES]
            acc = term if acc is None else acc + term
        d = acc.sum(axis=1, keepdims=True)
        act = 0.5 * d * (1.0 + lax.erf(d * (2.0 ** -0.5)))
        gcol = jnp.where(lane == t, gate_ref[...], 0.0).sum(axis=1, keepdims=True)
        w = gcol * act
        mix = []
        for c in range(half):
            v_c = buf[slot, pl.ds(half + c, PEER_PICKS, stride=UV_CHUNKS), :]
            mix.append((v_c * w).sum(axis=0, keepdims=True))
        return jnp.concatenate(mix, axis=1)

    def group(g, carry):
        t0 = pl.multiple_of(g * nbuf, nbuf)
        h_rows = h_ref[pl.ds(t0, nbuf), :]
        outs = []
        for j in range(nbuf):
            nxt = t0 + j + ahead

            @pl.when(nxt < tm)
            def _():
                issue(nxt, (j + ahead) % nbuf)

            outs.append(token(t0 + j, j, h_rows[j:j + 1]))
        o_ref[pl.ds(t0, nbuf), :] = x_ref[pl.ds(t0, nbuf), :] + g2_ref[0] * jnp.concatenate(outs, axis=0)
        return carry

    lax.fori_loop(0, tm // nbuf, group, 0)


def _peer_gather(idx_flat, gate_t, h2, x, g2, uv, tokens_per_batch, tok_off, n_tok, dep=None):
    D = h2.shape[1]
    tm = GATHER_TM
    off = tok_off // tm
    return pl.pallas_call(
        _gather_kernel,
        grid=(n_tok // tm,),
        in_specs=[pl.BlockSpec((tm * PEER_PICKS,), lambda i: (i + off,), memory_space=pltpu.SMEM),
                  pl.BlockSpec((PEER_PICKS, tm), lambda i: (0, i + off)),
                  pl.BlockSpec((tm, D), lambda i: (i + off, 0)),
                  pl.BlockSpec((tm, D), lambda i: (i + off, 0)),
                  pl.BlockSpec((1, 1, D), lambda i: ((tok_off + i * tm) // tokens_per_batch, 0, 0)),
                  pl.BlockSpec(memory_space=pl.ANY),
                  pl.BlockSpec(memory_space=pl.ANY)],
        out_specs=pl.BlockSpec((tm, D), lambda i: (i, 0)),
        out_shape=jax.ShapeDtypeStruct((n_tok, D), F32),
        scratch_shapes=[pltpu.VMEM((GATHER_NBUF, PEER_PICKS * UV_CHUNKS, LANES), F32),
                        pltpu.SemaphoreType.DMA((GATHER_NBUF,))],
        compiler_params=_cparams(("arbitrary",), 32),
        name="peer_gather",
    )(idx_flat, gate_t, h2, x, g2, uv, g2 if dep is None else dep)


SC_WORKERS = 32
SC_LANES = 16
SC_STREAM_PICKS = 32
SC_STREAMS = PEER_PICKS // SC_STREAM_PICKS
SC_INDEX_TOKENS = 128
SC_COL_UNROLL = 4


def _sc_params():
    cp = pltpu.CompilerParams()
    if "needs_layout_passes" in pltpu.CompilerParams.__dataclass_fields__:
        cp = dataclasses.replace(cp, needs_layout_passes=False)
    return cp


def _sc_token_loop(table_hbm, idx_hbm, idx_v, bufs, per, load_token, compute, store_token):
    wid = lax.axis_index("s") * 2 + lax.axis_index("c")
    base = wid * per
    sb = min(per, SC_INDEX_TOKENS)

    def start(i, ch, b):
        rows, sem = bufs[b]
        pltpu.async_copy(table_hbm.at[idx_v.at[pl.ds(i * PEER_PICKS + ch * SC_STREAM_PICKS, SC_STREAM_PICKS)]],
                         rows, sem)

    def wait(b):
        rows, sem = bufs[b]
        pltpu.make_async_copy(table_hbm.at[idx_v.at[pl.ds(0, SC_STREAM_PICKS)]], rows, sem).wait()

    @pl.loop(0, per // sb)
    def _(blk):
        tok0 = base + blk * sb
        pltpu.sync_copy(idx_hbm.at[pl.ds(tok0 * PEER_PICKS, sb * PEER_PICKS)], idx_v)
        start(0, 0, 0)
        start(0, 1, 1)

        @pl.loop(0, sb)
        def _(i):
            nxt = jnp.minimum(i + 1, sb - 1)
            load_token(tok0 + i)
            for ch in range(SC_STREAMS):
                b = ch % 2
                wait(b)
                compute(ch, bufs[b][0])
                if ch + 2 < SC_STREAMS:
                    start(i, ch + 2, b)
                else:
                    start(nxt, ch + 2 - SC_STREAMS, b)
            store_token(tok0 + i)

        wait(0)
        wait(1)


def _sc_scratch(per):
    sb = min(per, SC_INDEX_TOKENS)
    rows = pltpu.VMEM((SC_STREAM_PICKS, D_MODEL), F32)
    return [pltpu.VMEM((sb * PEER_PICKS,), jnp.int32), rows, rows, pltpu.SemaphoreType.DMA, pltpu.SemaphoreType.DMA]


def _sc_dots(u_tab, idx_flat, h_flat, tok_off, n_tok):
    per = n_tok // SC_WORKERS
    L = SC_LANES

    @functools.partial(
        pl.kernel, mesh=plsc.VectorSubcoreMesh(core_axis_name="c", subcore_axis_name="s"),
        out_type=jax.ShapeDtypeStruct((n_tok * PEER_PICKS,), F32),
        scratch_types=_sc_scratch(per) + [pltpu.VMEM((D_MODEL,), F32), pltpu.VMEM((PEER_PICKS,), F32)],
        compiler_params=_sc_params(),
    )
    def kern(u_hbm, idx_hbm, x_hbm, d_hbm, idx_v, rows0, rows1, sem0, sem1, x_v, d_v):
        lane = lax.iota(jnp.int32, L)

        def load_token(t):
            pltpu.sync_copy(x_hbm.at[tok_off + t], x_v)

        def compute(ch, rows):
            for g in range(SC_STREAM_PICKS // L):
                def cstep(cb, accs):
                    accs = list(accs)
                    for cc in range(SC_COL_UNROLL):
                        col = pl.ds((cb * SC_COL_UNROLL + cc) * L, L)
                        xv = x_v[col]
                        for j in range(L):
                            accs[j] = accs[j] + rows[g * L + j, col] * xv
                    return tuple(accs)

                accs = lax.fori_loop(0, D_MODEL // (L * SC_COL_UNROLL), cstep,
                                     tuple(jnp.zeros((L,), F32) for _ in range(L)))
                dvec = jnp.zeros((L,), F32)
                for j in range(L):
                    dvec = jnp.where(lane == j, jnp.sum(accs[j]), dvec)
                d_v[pl.ds(ch * SC_STREAM_PICKS + g * L, L)] = dvec

        def store_token(t):
            pltpu.sync_copy(d_v, d_hbm.at[pl.ds(t * PEER_PICKS, PEER_PICKS)])

        _sc_token_loop(u_hbm, idx_hbm, idx_v, ((rows0, sem0), (rows1, sem1)), per, load_token, compute,
                       store_token)

    return kern(u_tab, idx_flat, h_flat)


def _sc_mix(v_tab, idx_flat, w_flat, n_tok):
    per = n_tok // SC_WORKERS
    L = SC_LANES

    @functools.partial(
        pl.kernel, mesh=plsc.VectorSubcoreMesh(core_axis_name="c", subcore_axis_name="s"),
        out_type=jax.ShapeDtypeStruct((n_tok, D_MODEL), F32),
        scratch_types=_sc_scratch(per) + [pltpu.VMEM((PEER_PICKS,), F32), pltpu.VMEM((D_MODEL,), F32)],
        compiler_params=_sc_params(),
    )
    def kern(v_hbm, idx_hbm, w_hbm, o_hbm, idx_v, rows0, rows1, sem0, sem1, w_v, o_v):
        def load_token(t):
            pltpu.sync_copy(w_hbm.at[pl.ds(t * PEER_PICKS, PEER_PICKS)], w_v)

        def compute(ch, rows):
            for cg in range(D_MODEL // (L * L)):
                cols = [pl.ds((cg * L + c) * L, L) for c in range(L)]
                init = (tuple(jnp.zeros((L,), F32) for _ in cols) if ch == 0 else tuple(o_v[c] for c in cols))

                def kstep(k, accs):
                    wk = plsc.load_gather(w_v, [jnp.full((L,), ch * SC_STREAM_PICKS, jnp.int32) + k])
                    return tuple(acc + rows[k, c] * wk for acc, c in zip(accs, cols))

                accs = lax.fori_loop(0, SC_STREAM_PICKS, kstep, init)
                for acc, c in zip(accs, cols):
                    o_v[c] = acc

        def store_token(t):
            pltpu.sync_copy(o_v, o_hbm.at[t])

        _sc_token_loop(v_hbm, idx_hbm, idx_v, ((rows0, sem0), (rows1, sem1)), per, load_token, compute,
                       store_token)

    return kern(v_tab, idx_flat, w_flat)


def _act_kernel(d_ref, g_ref, dep_ref, w_ref):
    d = d_ref[...]
    w_ref[...] = g_ref[...] * (0.5 * d * (1.0 + lax.erf(d * (2.0 ** -0.5))))


def _peer_act(d, gate, dep):
    n = d.shape[0]
    tm = min(n, 2048)
    blk = pl.BlockSpec((tm, PEER_PICKS), lambda i: (i, 0))
    return pl.pallas_call(
        _act_kernel, grid=(n // tm,),
        in_specs=[blk, blk, pl.BlockSpec(memory_space=pl.ANY)], out_specs=blk,
        out_shape=jax.ShapeDtypeStruct(d.shape, F32),
        compiler_params=_cparams(("arbitrary",), 32), name="peer_act",
    )(d, gate, dep)


def _residual_kernel(x_ref, g2_ref, m_ref, o_ref):
    o_ref[...] = x_ref[...] + g2_ref[0] * m_ref[...]


def _peer_residual(x, g2, mix, tok_off, tokens_per_batch):
    n, D = mix.shape
    tm = min(n, 512)
    off = tok_off // tm
    return pl.pallas_call(
        _residual_kernel, grid=(n // tm,),
        in_specs=[pl.BlockSpec((tm, D), lambda i: (i + off, 0)),
                  pl.BlockSpec((1, 1, D), lambda i: ((tok_off + i * tm) // tokens_per_batch, 0, 0)),
                  pl.BlockSpec((tm, D), lambda i: (i, 0))],
        out_specs=pl.BlockSpec((tm, D), lambda i: (i, 0)),
        out_shape=jax.ShapeDtypeStruct((n, D), F32),
        compiler_params=_cparams(("arbitrary",), 32), name="peer_residual",
    )(x, g2, mix)


def _sc_share(n_tok):
    unit = 2 * SC_WORKERS * SC_INDEX_TOKENS
    return n_tok // 2 if n_tok % unit == 0 else 0


def _peer(h2, x, g2, p, sc_tokens=0):
    B, T, D = x.shape
    n = B * T
    hf = h2.reshape(n, D)
    xf = x.reshape(n, D)
    idx_t, gate_t = _peer_route(hf, p)
    n_tc = n - sc_tokens
    if sc_tokens == 0:
        out = _peer_gather(idx_t.T.reshape(-1), gate_t, hf, xf, g2, p["uv"], T, 0, n)
        return out.reshape(B, T, D)
    idx_sc = idx_t[:, n_tc:].T.reshape(-1)
    gate_sc = gate_t[:, n_tc:].T
    idx_tc = idx_t[:, :n_tc].T.reshape(-1)
    half = n_tc // 2
    d = _sc_dots(p["peer_u"], idx_sc, hf, n_tc, sc_tokens)
    out_a = _peer_gather(idx_tc, gate_t, hf, xf, g2, p["uv"], T, 0, half)
    w = _peer_act(d.reshape(sc_tokens, PEER_PICKS), gate_sc, out_a)
    mix = _sc_mix(p["peer_v"], idx_sc, w.reshape(-1), sc_tokens)
    out_b = _peer_gather(idx_tc, gate_t, hf, xf, g2, p["uv"], T, half, n_tc - half, dep=w)
    out_sc = _peer_residual(xf, g2, mix, n_tc, T)
    return jnp.concatenate([out_a, out_b, out_sc], axis=0).reshape(B, T, D)


def _rearrange_w_in(w):
    cols = [w[:, o:o + n] for _, o, n, _ in _SEGS]
    cols.append(jnp.zeros((w.shape[0], PROJ_W - sum(n for _, _, n, _ in _SEGS)), w.dtype))
    return jnp.concatenate(cols, axis=1).astype(BF16)


def _rope_tables(S):
    t = jnp.arange(S)
    row = (t // GRID_W).astype(F32)[:, None]
    colp = (t % GRID_W).astype(F32)[:, None]
    nf = HEAD_DIM // 4
    inv = ROPE_THETA ** (-jnp.arange(nf, dtype=F32) / nf)
    ang = jnp.concatenate([row * inv, row * inv, colp * inv, colp * inv], axis=-1)
    ang = jnp.concatenate([ang, ang], axis=-1)
    return jnp.cos(ang), jnp.sin(ang)


def _layer_params(i, a):
    two = lambda g: jnp.concatenate([g, g]).reshape(1, LANES)
    hid = np.arange(LANES) // HEAD_DIM
    E = PEER_EXPERTS
    w_pq_hi, w_pq_lo = _split_bf16(a["w_pq"][i])
    keys_hi, keys_lo = _split_bf16(a["sub_keys"][i])
    return {
        "w_mod": a["w_mod"][i], "b_mod": a["b_mod"][i],
        "norm1": a["norm1"][i], "norm2": a["norm2"][i].reshape(1, D_MODEL),
        "w_in": _rearrange_w_in(a["w_in"][i]),
        "qn_a2": two(a["qn_a"][i]), "kn_a2": two(a["kn_a"][i]),
        "qn_b2": two(a["qn_b"][i]), "kn_b2": two(a["kn_b"][i]),
        "m128": jnp.asarray(hid[:, None] == hid[None, :], BF16),
        "sink_a": a["sink_a"][i], "rpb_b": a["rpb_b"][i],
        "wa2_f": a["wa2_f"][i], "wa2_b": a["wa2_b"][i],
        "ba_f": a["ba_f"][i].reshape(1, -1), "ba_b": a["ba_b"][i].reshape(1, -1),
        "gn_c": a["gn_c"][i].reshape(1, C_DV),
        "w_oa": a["w_oa"][i].astype(BF16), "w_ob": a["w_ob"][i].astype(BF16),
        "w_oc": a["w_oc"][i].astype(BF16), "w_out": a["w_out"][i].astype(BF16),
        "w_pq_hi": w_pq_hi, "w_pq_lo": w_pq_lo, "pq_g": a["pq_g"][i].reshape(1, PEER_HALF),
        "sub_keys_hi": keys_hi, "sub_keys_lo": keys_lo,
        "peer_u": a["peer_u"][i], "peer_v": a["peer_v"][i],
        "uv": jnp.concatenate([a["peer_u"][i].reshape(E, UV_CHUNKS // 2, LANES),
                               a["peer_v"][i].reshape(E, UV_CHUNKS // 2, LANES)],
                              axis=1).reshape(E * UV_CHUNKS, LANES),
    }


def _layer(x, ctx, c_all, p, cos, sin, last):
    B, S, D = x.shape
    L = ctx.shape[1]
    ones = jnp.ones((L, LANES), F32)
    zeros = jnp.zeros((L, LANES), F32)
    state0 = jnp.zeros((B, C_HEADS, C_DV, C_DK), F32)
    mods = _modulation(c_all, p["w_mod"], p["b_mod"])
    lat = [mods[:B, k * D:(k + 1) * D].reshape(B, 1, D) for k in range(N_MOD)]
    cm = [jnp.broadcast_to(mods[B, k * D:(k + 1) * D].reshape(1, 1, D), (B, 1, D)) for k in range(N_MOD)]
    sh1, sc1, g1, sh2, sc2, g2 = lat
    csh1, csc1, cg1, csh2, csc2, cg2 = cm

    proj = _norm_proj(x, p["norm1"], sh1, sc1, p["w_in"])
    cproj = _norm_proj(ctx, p["norm1"], csh1, csc1, p["w_in"])
    qa, ka, qb, kb = _qk_prep(proj, cos, sin, p)
    cqa, cka, cqb, ckb = _qk_prep(cproj, ones, zeros, p)

    o_a = _attn_a(qa, ka, proj, cka, cproj, p["sink_a"])
    o_b = _attn_b(qb, kb, proj, ckb, cproj, _na_bias(p["rpb_b"], S // GRID_W))
    co_f, co_b, s_f, s_b = _gla(cproj, p, state0, state0)
    o_f, o_bk, _, _ = _gla(proj, p, s_f, s_b)
    x, h2 = _merge(o_a, o_b, o_f, o_bk, proj, x, g1, sh2, sc2, p)
    x = _peer(h2, x, g2, p, sc_tokens=_sc_share(B * S))
    if not last:
        co_a = _ctx_attn(cqa, cka, cproj, "vA", p["sink_a"], A_KV_HEADS, True)
        co_bb = _ctx_attn(cqb, ckb, cproj, "vB", p["sink_a"], B_HEADS, False)
        ctx, hc2 = _merge(co_a, co_bb, co_f, co_b, cproj, ctx, cg1, csh2, csc2, p)
        ctx = _peer(hc2, ctx, cg2, p)
    return x, ctx


def _forward(a):
    x, ctx = a["x"], a["ctx"]
    B, S, D = x.shape
    cos, sin = _rope_tables(S)
    c_all = jnp.zeros((8, D), F32).at[:B].set(a["c"]).at[B].set(a["c_ctx"])
    for i in range(DEPTH):
        x, ctx = _layer(x, ctx, c_all, _layer_params(i, a), cos, sin, i == DEPTH - 1)
    return x


def kernel(x, c, ctx, c_ctx, w_mod, b_mod, norm1, norm2, w_in, qn_a, kn_a, sink_a, qn_b, kn_b, rpb_b,
           wa2_f, ba_f, wa2_b, ba_b, gn_c, w_oa, w_ob, w_oc, w_out, w_pq, pq_g, sub_keys, peer_u, peer_v):
    return _forward(dict(
        x=x, c=c, ctx=ctx, c_ctx=c_ctx, w_mod=w_mod, b_mod=b_mod, norm1=norm1, norm2=norm2, w_in=w_in,
        qn_a=qn_a, kn_a=kn_a, sink_a=sink_a, qn_b=qn_b, kn_b=kn_b, rpb_b=rpb_b, wa2_f=wa2_f, ba_f=ba_f,
        wa2_b=wa2_b, ba_b=ba_b, gn_c=gn_c, w_oa=w_oa, w_ob=w_ob, w_oc=w_oc, w_out=w_out, w_pq=w_pq,
        pq_g=pq_g, sub_keys=sub_keys, peer_u=peer_u, peer_v=peer_v))
```

```python
import dataclasses
import functools

import numpy as np
import jax
import jax.numpy as jnp
from jax import lax
from jax.experimental import pallas as pl
from jax.experimental.pallas import tpu as pltpu
from jax.experimental.pallas import tpu_sc as plsc

F32 = jnp.float32
BF16 = jnp.bfloat16

D_MODEL = 1024
DEPTH = 4
GRID_W = 64
HEAD_DIM = 64
ROPE_THETA = 10000.0
EPS = 1e-6
NEG = -1e30
N_MOD = 6
A_HEADS = 8
A_KV_HEADS = 2
A_BLOCK = 128
B_HEADS = 8
NA_ROWS = 8
NA_COLS = 16
C_HEADS = 4
C_DK = 128
C_DV = 256
C_GATE_RANK = 16
C_GATE_TAU = 16.0
C_CHUNK = 64
PEER_HEADS = 8
PEER_NKEYS = 128
PEER_EXPERTS = PEER_NKEYS * PEER_NKEYS
PEER_HALF = 128
PEER_TOPK = 16
PEER_PICKS = PEER_HEADS * PEER_TOPK

LANES = 128
MIB = 1024 * 1024

_SEGS = (
    ("vC", 1792, 1024, 0), ("rC", 4384, 1024, 1024), ("gA", 5408, 1024, 2048),
    ("gB", 6432, 1024, 3072), ("gC", 7456, 1024, 4096), ("kB", 256, 512, 5120),
    ("vB", 768, 512, 5632), ("kC", 1280, 512, 6144), ("qA", 2848, 512, 6656),
    ("qB", 3360, 512, 7168), ("qC", 3872, 512, 7680), ("kA", 0, 128, 8192),
    ("vA", 128, 128, 8320), ("aC", 2816, 32, 8448),
)
PROJ_W = 8704
PROJ_TN = 2176
COLBLK = {name: new // (LANES if w < LANES else w) for name, _, w, new in _SEGS}
PROJ_DT = BF16

NA_TR = 4
GATHER_TM = 128
GATHER_NBUF = 8
UV_CHUNKS = 16


def _cparams(sem, vmem_mib):
    return pltpu.CompilerParams(dimension_semantics=sem, vmem_limit_bytes=vmem_mib * MIB)


def _split_bf16(a):
    hi = a.astype(BF16)
    lo = (a - hi.astype(F32)).astype(BF16)
    return hi, lo


def _dot(a, b):
    return jnp.dot(a, b, preferred_element_type=F32)


def _dot_nt(a, b):
    return lax.dot_general(a, b, (((1,), (1,)), ((), ())), preferred_element_type=F32)


def _dot_tn(a, b):
    return lax.dot_general(a, b, (((0,), (0,)), ((), ())), preferred_element_type=F32)


def _dot_precise(a, b):
    ah, al = _split_bf16(a)
    bh, bl = _split_bf16(b)
    return _dot(ah, bh) + _dot(ah, bl) + _dot(al, bh)


def _mod_kernel(c_ref, w_ref, b_ref, o_ref):
    c = c_ref[...]
    act = c * jax.nn.sigmoid(c)
    o_ref[...] = _dot_precise(act, w_ref[...]) + b_ref[...]


def _modulation(c_all, w_mod, b_mod):
    n = w_mod.shape[1]
    tn = D_MODEL
    return pl.pallas_call(
        _mod_kernel,
        grid=(n // tn,),
        in_specs=[pl.BlockSpec((8, D_MODEL), lambda j: (0, 0)),
                  pl.BlockSpec((D_MODEL, tn), lambda j: (0, j)),
                  pl.BlockSpec((1, tn), lambda j: (0, j))],
        out_specs=pl.BlockSpec((8, tn), lambda j: (0, j)),
        out_shape=jax.ShapeDtypeStruct((8, n), F32),
        compiler_params=_cparams(("arbitrary",), 32),
        name="modulation",
    )(c_all, w_mod, b_mod.reshape(1, n))


def _modulated_norm(x, g, sh, sc):
    ms = jnp.mean(x * x, axis=-1, keepdims=True)
    return (x * lax.rsqrt(ms + EPS) * g) * (1.0 + sc) + sh


def _norm_proj_kernel(x_ref, g_ref, sh_ref, sc_ref, w_ref, o_ref):
    h = _modulated_norm(x_ref[0], g_ref[...], sh_ref[0], sc_ref[0])
    o_ref[0] = _dot(h.astype(BF16), w_ref[...]).astype(o_ref.dtype)


def _norm_proj(x, g, sh, sc, w):
    B, T, D = x.shape
    tm = min(T, 512)
    return pl.pallas_call(
        _norm_proj_kernel,
        grid=(PROJ_W // PROJ_TN, B, T // tm),
        in_specs=[pl.BlockSpec((1, tm, D), lambda j, b, i: (b, i, 0)),
                  pl.BlockSpec((1, D), lambda j, b, i: (0, 0)),
                  pl.BlockSpec((1, 1, D), lambda j, b, i: (b, 0, 0)),
                  pl.BlockSpec((1, 1, D), lambda j, b, i: (b, 0, 0)),
                  pl.BlockSpec((D, PROJ_TN), lambda j, b, i: (0, j))],
        out_specs=pl.BlockSpec((1, tm, PROJ_TN), lambda j, b, i: (b, i, j)),
        out_shape=jax.ShapeDtypeStruct((B, T, PROJ_W), PROJ_DT),
        compiler_params=_cparams(("arbitrary", "arbitrary", "arbitrary"), 48),
        name="norm_proj",
    )(x, g.reshape(1, D), sh, sc, w)


def _head_norm(z, g2, m128):
    hi, lo = _split_bf16(z * z)
    ss = _dot(hi, m128) + _dot(lo, m128)
    return z * lax.rsqrt(ss * (1.0 / HEAD_DIM) + EPS) * g2


def _rope128(z, cos, sin, even16):
    rot = jnp.where(even16, -pltpu.roll(z, LANES - 16, 1), pltpu.roll(z, 16, 1))
    return z * cos + rot * sin


def _prep_kernel(qa_ref, qb_ref, kb_ref, ka_ref, cos_ref, sin_ref, gqa_ref, gka_ref, gqb_ref,
                 gkb_ref, m_ref, oqa_ref, oka_ref, oqb_ref, okb_ref):
    m128 = m_ref[...]
    cos = cos_ref[...]
    sin = sin_ref[...]
    lane = lax.broadcasted_iota(jnp.int32, cos.shape, 1)
    even16 = ((lane // 16) % 2) == 0
    scale = HEAD_DIM ** -0.5
    for s in range(4):
        sl = slice(s * LANES, (s + 1) * LANES)
        za = _head_norm(qa_ref[0][:, sl].astype(F32), gqa_ref[...], m128)
        oqa_ref[0, :, sl] = (_rope128(za, cos, sin, even16) * scale).astype(BF16)
        zb = _head_norm(qb_ref[0][:, sl].astype(F32), gqb_ref[...], m128)
        oqb_ref[0, :, sl] = (zb * scale).astype(BF16)
        zk = _head_norm(kb_ref[0][:, sl].astype(F32), gkb_ref[...], m128)
        okb_ref[0, :, sl] = zk.astype(BF16)
    zk = _head_norm(ka_ref[0].astype(F32), gka_ref[...], m128)
    oka_ref[0] = _rope128(zk, cos, sin, even16).astype(BF16)


def _qk_prep(proj, cos, sin, p):
    B, T, _ = proj.shape
    tm = min(T, 256)

    def col(name, w):
        blk = COLBLK[name]
        return pl.BlockSpec((1, tm, w), lambda b, i: (b, i, blk))

    vec = pl.BlockSpec((1, LANES), lambda b, i: (0, 0))
    tab = pl.BlockSpec((tm, LANES), lambda b, i: (i, 0))
    out = lambda w: pl.BlockSpec((1, tm, w), lambda b, i: (b, i, 0))
    return pl.pallas_call(
        _prep_kernel,
        grid=(B, T // tm),
        in_specs=[col("qA", 512), col("qB", 512), col("kB", 512), col("kA", 128), tab, tab,
                  vec, vec, vec, vec, pl.BlockSpec((LANES, LANES), lambda b, i: (0, 0))],
        out_specs=[out(512), out(128), out(512), out(512)],
        out_shape=[jax.ShapeDtypeStruct((B, T, 512), BF16), jax.ShapeDtypeStruct((B, T, 128), BF16),
                   jax.ShapeDtypeStruct((B, T, 512), BF16), jax.ShapeDtypeStruct((B, T, 512), BF16)],
        compiler_params=_cparams(("arbitrary", "arbitrary"), 32),
        name="qk_prep",
    )(proj, proj, proj, proj, cos, sin, p["qn_a2"], p["kn_a2"], p["qn_b2"], p["kn_b2"], p["m128"])


def _softmax_pv(parts, sink):
    m = parts[0][0].max(axis=-1, keepdims=True)
    for s, _ in parts[1:]:
        m = jnp.maximum(m, s.max(axis=-1, keepdims=True))
    if sink is not None:
        m = jnp.maximum(m, sink)
    den = None
    acc = None
    for s, v in parts:
        e = jnp.exp(s - m)
        ds = e.sum(axis=-1, keepdims=True)
        den = ds if den is None else den + ds
        pv = _dot(e.astype(BF16), v)
        acc = pv if acc is None else acc + pv
    if sink is not None:
        den = den + jnp.exp(sink - m)
    return acc / den


def _attn_a_kernel(sink_ref, q_ref, kp_ref, ko_ref, kn_ref, vp_ref, vo_ref, vn_ref, ck_ref, cv_ref, o_ref):
    n = pl.program_id(1)
    nb = pl.num_programs(1)
    q = q_ref[0]
    k_loc = jnp.concatenate([kp_ref[0], ko_ref[0], kn_ref[0]], axis=0)
    v_loc = jnp.concatenate([vp_ref[0], vo_ref[0], vn_ref[0]], axis=0).astype(BF16)
    ck = ck_ref[0]
    cv = cv_ref[0].astype(BF16)
    qi = lax.broadcasted_iota(jnp.int32, (A_BLOCK, 3 * A_BLOCK), 0)
    kc = lax.broadcasted_iota(jnp.int32, (A_BLOCK, 3 * A_BLOCK), 1)
    rel = kc - qi
    valid = (rel >= 0) & (rel <= 2 * A_BLOCK)
    valid = valid & ((n > 0) | (kc >= A_BLOCK)) & ((n < nb - 1) | (kc < 2 * A_BLOCK))
    rep = A_HEADS // A_KV_HEADS
    for h in range(A_HEADS):
        g = h // rep
        qh = q[:, h * HEAD_DIM:(h + 1) * HEAD_DIM]
        gs = slice(g * HEAD_DIM, (g + 1) * HEAD_DIM)
        s_loc = jnp.where(valid, _dot_nt(qh, k_loc[:, gs]), NEG)
        s_ctx = _dot_nt(qh, ck[:, gs])
        o = _softmax_pv([(s_loc, v_loc[:, gs]), (s_ctx, cv[:, gs])], sink_ref[h])
        o_ref[0, :, h * HEAD_DIM:(h + 1) * HEAD_DIM] = o.astype(o_ref.dtype)


def _attn_a(qa, ka, proj, cka, cproj, sink):
    B, S, _ = qa.shape
    L = cka.shape[1]
    nb = S // A_BLOCK
    vblk = COLBLK["vA"]
    blk = lambda f: pl.BlockSpec((1, A_BLOCK, 128), f)
    return pl.pallas_call(
        _attn_a_kernel,
        grid=(B, nb),
        in_specs=[pl.BlockSpec(memory_space=pltpu.SMEM),
                  pl.BlockSpec((1, A_BLOCK, 512), lambda b, n: (b, n, 0)),
                  blk(lambda b, n: (b, jnp.maximum(n - 1, 0), 0)),
                  blk(lambda b, n: (b, n, 0)),
                  blk(lambda b, n: (b, jnp.minimum(n + 1, nb - 1), 0)),
                  blk(lambda b, n: (b, jnp.maximum(n - 1, 0), vblk)),
                  blk(lambda b, n: (b, n, vblk)),
                  blk(lambda b, n: (b, jnp.minimum(n + 1, nb - 1), vblk)),
                  pl.BlockSpec((1, L, 128), lambda b, n: (b, 0, 0)),
                  pl.BlockSpec((1, L, 128), lambda b, n: (b, 0, vblk))],
        out_specs=pl.BlockSpec((1, A_BLOCK, 512), lambda b, n: (b, n, 0)),
        out_shape=jax.ShapeDtypeStruct((B, S, 512), BF16),
        compiler_params=_cparams(("arbitrary", "arbitrary"), 32),
        name="attn_window",
    )(sink, qa, ka, ka, ka, proj, proj, proj, cka, cproj)


def _attn_b_kernel(q_ref, k0_ref, k1_ref, k2_ref, v0_ref, v1_ref, v2_ref, ck_ref, cv_ref, bias_ref, o_ref):
    q = q_ref[0]
    k_loc = jnp.concatenate([k0_ref[0], k1_ref[0], k2_ref[0]], axis=0)
    v_loc = jnp.concatenate([v0_ref[0], v1_ref[0], v2_ref[0]], axis=0).astype(BF16)
    ck = ck_ref[0]
    cv = cv_ref[0].astype(BF16)
    for h in range(B_HEADS):
        hs = slice(h * HEAD_DIM, (h + 1) * HEAD_DIM)
        qh = q[:, hs]
        s_loc = _dot_nt(qh, k_loc[:, hs]) + bias_ref[0, h]
        s_ctx = _dot_nt(qh, ck[:, hs])
        o = _softmax_pv([(s_loc, v_loc[:, hs]), (s_ctx, cv[:, hs])], None)
        o_ref[0, :, hs] = o.astype(o_ref.dtype)


def _na_bias(rpb, rows):
    c = np.arange(GRID_W)
    col_start = np.clip(c - NA_COLS // 2, 0, GRID_W - NA_COLS)
    col_valid = (c[None, :] >= col_start[:, None]) & (c[None, :] < col_start[:, None] + NA_COLS)
    dc = np.clip(c[None, :] - c[:, None], -(NA_COLS - 1), NA_COLS - 1) + (NA_COLS - 1)
    toe = jnp.where(jnp.asarray(col_valid)[None, None], jnp.take(rpb, jnp.asarray(dc.reshape(-1)), axis=2)
                    .reshape(B_HEADS, 2 * NA_ROWS - 1, GRID_W, GRID_W), NEG)
    masked = jnp.full((B_HEADS, GRID_W, GRID_W), NEG, F32)
    tiles = []
    for r0 in (0, NA_TR, rows - NA_TR):
        band = []
        for rq in range(r0, r0 + NA_TR):
            start = min(max(rq - NA_ROWS // 2, 0), rows - NA_ROWS)
            blocks = [toe[:, kr - rq + NA_ROWS - 1] if start <= kr < start + NA_ROWS else masked
                      for kr in range(r0 - NA_TR, r0 + 2 * NA_TR)]
            band.append(jnp.concatenate(blocks, axis=2))
        tiles.append(jnp.concatenate(band, axis=1))
    return jnp.stack(tiles).astype(F32)


def _attn_b(qb, kb, proj, ckb, cproj, bias):
    B, S, _ = qb.shape
    L = ckb.shape[1]
    tq = NA_TR * GRID_W
    ns = S // tq
    vblk = COLBLK["vB"]
    blk = lambda f: pl.BlockSpec((1, tq, 512), f)
    lo = lambda b, i: (b, jnp.maximum(i - 1, 0), 0)
    hi = lambda b, i: (b, jnp.minimum(i + 1, ns - 1), 0)
    variant = lambda b, i: (jnp.where(i == 0, 0, jnp.where(i == ns - 1, 2, 1)), 0, 0, 0)
    return pl.pallas_call(
        _attn_b_kernel,
        grid=(B, ns),
        in_specs=[blk(lambda b, i: (b, i, 0)),
                  blk(lo), blk(lambda b, i: (b, i, 0)), blk(hi),
                  blk(lambda b, i: lo(b, i)[:2] + (vblk,)),
                  blk(lambda b, i: (b, i, vblk)),
                  blk(lambda b, i: hi(b, i)[:2] + (vblk,)),
                  pl.BlockSpec((1, L, 512), lambda b, i: (b, 0, 0)),
                  pl.BlockSpec((1, L, 512), lambda b, i: (b, 0, vblk)),
                  pl.BlockSpec((1, B_HEADS, tq, 3 * tq), variant)],
        out_specs=blk(lambda b, i: (b, i, 0)),
        out_shape=jax.ShapeDtypeStruct((B, S, 512), BF16),
        compiler_params=_cparams(("arbitrary", "arbitrary"), 56),
        name="attn_neighbourhood",
    )(qb, kb, kb, kb, proj, proj, proj, ckb, cproj, bias)


def _ctx_attn_kernel(sink_ref, q_ref, k_ref, v_ref, o_ref, *, kv_heads, use_sink):
    q = q_ref[0]
    k = k_ref[0]
    v = v_ref[0].astype(BF16)
    rep = A_HEADS // kv_heads
    for h in range(A_HEADS):
        g = h // rep
        gs = slice(g * HEAD_DIM, (g + 1) * HEAD_DIM)
        s = _dot_nt(q[:, h * HEAD_DIM:(h + 1) * HEAD_DIM], k[:, gs])
        o = _softmax_pv([(s, v[:, gs])], sink_ref[h] if use_sink else None)
        o_ref[0, :, h * HEAD_DIM:(h + 1) * HEAD_DIM] = o.astype(o_ref.dtype)


def _ctx_attn(q, k, cproj, vname, sink, kv_heads, use_sink):
    B, L, _ = q.shape
    kw = kv_heads * HEAD_DIM
    vblk = COLBLK[vname]
    return pl.pallas_call(
        functools.partial(_ctx_attn_kernel, kv_heads=kv_heads, use_sink=use_sink),
        grid=(B,),
        in_specs=[pl.BlockSpec(memory_space=pltpu.SMEM),
                  pl.BlockSpec((1, L, 512), lambda b: (b, 0, 0)),
                  pl.BlockSpec((1, L, kw), lambda b: (b, 0, 0)),
                  pl.BlockSpec((1, L, kw), lambda b: (b, 0, vblk))],
        out_specs=pl.BlockSpec((1, L, 512), lambda b: (b, 0, 0)),
        out_shape=jax.ShapeDtypeStruct((B, L, 512), BF16),
        compiler_params=_cparams(("arbitrary",), 32),
        name="attn_context",
    )(sink, q, k, cproj)


def _gla_direction(q_ref, k_ref, v_ref, a_ref, w2_ref, b_ref, o_ref, S, forward):
    C = q_ref.shape[1]
    lo = 0 if forward else C_GATE_RANK
    a = a_ref[0][:, lo:lo + C_GATE_RANK].astype(F32)
    z = _dot_precise(a, w2_ref[...]) + b_ref[...]
    la = (jnp.minimum(z, 0.0) - jnp.log1p(jnp.exp(-jnp.abs(z)))) * (1.0 / C_GATE_TAU)
    ri = lax.broadcasted_iota(jnp.int32, (C, C), 0)
    ci = lax.broadcasted_iota(jnp.int32, (C, C), 1)
    keep = (ci <= ri) if forward else (ci >= ri)
    tri = keep.astype(BF16)
    la_hi, la_lo = _split_bf16(la)
    cum = _dot(tri, la_hi) + _dot(tri, la_lo)
    ref_row = C // 2 - 1 if forward else C // 2
    end_row = C - 1 if forward else 0
    scale = C_DK ** -0.5
    for h in range(C_HEADS):
        ks = slice(h * C_DK, (h + 1) * C_DK)
        vs = slice(h * C_DV, (h + 1) * C_DV)
        c = cum[:, ks]
        c_ref = c[ref_row:ref_row + 1]
        c_end = c[end_row:end_row + 1]
        q = q_ref[0][:, ks].astype(F32) * scale
        k = k_ref[0][:, ks].astype(F32)
        v = v_ref[0][:, vs].astype(BF16)
        att = _dot_nt((q * jnp.exp(c - c_ref)).astype(BF16), (k * jnp.exp(c_ref - c)).astype(BF16))
        att = jnp.where(keep, att, 0.0)
        st = S[h]
        o = _dot(att.astype(BF16), v) + _dot_nt((q * jnp.exp(c)).astype(BF16), st.astype(BF16))
        kd = (k * jnp.exp(c_end - c)).astype(BF16)
        S[h] = st * jnp.exp(c_end) + _dot_tn(v, kd)
        o_ref[0, :, vs] = o


def _gla_kernel(qf, kf, vf, af, qb, kb, vb, ab, w2f, w2b, bf, bb, s0f, s0b, of, ob, sff, sfb, Sf, Sb):
    j = pl.program_id(1)

    @pl.when(j == 0)
    def _():
        Sf[...] = s0f[0]
        Sb[...] = s0b[0]

    _gla_direction(qf, kf, vf, af, w2f, bf, of, Sf, True)
    _gla_direction(qb, kb, vb, ab, w2b, bb, ob, Sb, False)

    @pl.when(j == pl.num_programs(1) - 1)
    def _():
        sff[0] = Sf[...]
        sfb[0] = Sb[...]


def _gla(proj, p, s0f, s0b):
    B, T, _ = proj.shape
    C = C_CHUNK
    n = T // C
    fwd = lambda b, j: (b, j)
    bwd = lambda b, j: (b, n - 1 - j)

    def cols(order):
        mk = lambda name, w: pl.BlockSpec((1, C, w), lambda b, j: order(b, j) + (COLBLK[name],))
        return [mk("qC", 512), mk("kC", 512), mk("vC", 1024), mk("aC", 128)]

    wspec = pl.BlockSpec((C_GATE_RANK, 512), lambda b, j: (0, 0))
    bspec = pl.BlockSpec((1, 512), lambda b, j: (0, 0))
    sspec = pl.BlockSpec((1, C_HEADS, C_DV, C_DK), lambda b, j: (b, 0, 0, 0))
    st = jax.ShapeDtypeStruct((B, C_HEADS, C_DV, C_DK), F32)
    return pl.pallas_call(
        _gla_kernel,
        grid=(B, n),
        in_specs=cols(fwd) + cols(bwd) + [wspec, wspec, bspec, bspec, sspec, sspec],
        out_specs=[pl.BlockSpec((1, C, 1024), lambda b, j: (b, j, 0)),
                   pl.BlockSpec((1, C, 1024), lambda b, j: (b, n - 1 - j, 0)),
                   sspec, sspec],
        out_shape=[jax.ShapeDtypeStruct((B, T, 1024), F32), jax.ShapeDtypeStruct((B, T, 1024), F32), st, st],
        scratch_shapes=[pltpu.VMEM((C_HEADS, C_DV, C_DK), F32), pltpu.VMEM((C_HEADS, C_DV, C_DK), F32)],
        compiler_params=_cparams(("arbitrary", "arbitrary"), 32),
        name="gla_scan",
    )(proj, proj, proj, proj, proj, proj, proj, proj, p["wa2_f"], p["wa2_b"], p["ba_f"], p["ba_b"], s0f, s0b)


def _merge_kernel(oa_ref, ob_ref, of_ref, obk_ref, r_ref, ga_ref, gb_ref, gc_ref, x_ref, g1_ref, sh_ref,
                  sc_ref, n2_ref, gn_ref, woa_ref, wob_ref, woc_ref, wout_ref, xo_ref, h2_ref):
    o = of_ref[0] + obk_ref[0]
    r = r_ref[0].astype(F32)
    parts = []
    for h in range(C_HEADS):
        vs = slice(h * C_DV, (h + 1) * C_DV)
        oh = o[:, vs]
        ms = jnp.mean(oh * oh, axis=-1, keepdims=True)
        rh = r[:, vs]
        parts.append((oh * lax.rsqrt(ms + EPS) * gn_ref[...]) * (rh * jax.nn.sigmoid(rh)))
    oc = jnp.concatenate(parts, axis=-1).astype(BF16)
    y = (jax.nn.sigmoid(ga_ref[0].astype(F32)) * _dot(oa_ref[0], woa_ref[...])
         + jax.nn.sigmoid(gb_ref[0].astype(F32)) * _dot(ob_ref[0], wob_ref[...])
         + jax.nn.sigmoid(gc_ref[0].astype(F32)) * _dot(oc, woc_ref[...]))
    xn = x_ref[0] + g1_ref[0] * _dot(y.astype(BF16), wout_ref[...])
    xo_ref[0] = xn
    h2_ref[0] = _modulated_norm(xn, n2_ref[...], sh_ref[0], sc_ref[0])


def _merge(oa, ob, of, obk, proj, x, g1, sh2, sc2, p):
    B, T, D = x.shape
    tm = min(T, 256)
    tok = lambda w: pl.BlockSpec((1, tm, w), lambda b, i: (b, i, 0))
    col = lambda name: pl.BlockSpec((1, tm, 1024), lambda b, i: (b, i, COLBLK[name]))
    mod = pl.BlockSpec((1, 1, D), lambda b, i: (b, 0, 0))
    full = lambda r, c: pl.BlockSpec((r, c), lambda b, i: (0, 0))
    return pl.pallas_call(
        _merge_kernel,
        grid=(B, T // tm),
        in_specs=[tok(512), tok(512), tok(1024), tok(1024), col("rC"), col("gA"), col("gB"), col("gC"),
                  tok(D), mod, mod, mod, full(1, D), full(1, C_DV),
                  full(512, D), full(512, D), full(1024, D), full(D, D)],
        out_specs=[tok(D), tok(D)],
        out_shape=[jax.ShapeDtypeStruct((B, T, D), F32), jax.ShapeDtypeStruct((B, T, D), F32)],
        compiler_params=_cparams(("arbitrary", "arbitrary"), 56),
        name="merge",
    )(oa, ob, of, obk, proj, proj, proj, proj, x, g1, sh2, sc2, p["norm2"], p["gn_c"],
      p["w_oa"], p["w_ob"], p["w_oc"], p["w_out"])


def _topk_rows(arr, rowid, k, payload=None):
    n = arr.shape[0]
    vals, ids = [], []
    for _ in range(k):
        m = arr.max(axis=0, keepdims=True)
        sel = jnp.where(arr == m, rowid, float(n)).min(axis=0, keepdims=True)
        hit = rowid == sel
        vals.append(m)
        ids.append(sel if payload is None else jnp.where(hit, payload, -1.0).max(axis=0, keepdims=True))
        arr = jnp.where(hit, -jnp.inf, arr)
    return jnp.concatenate(vals, axis=0), jnp.concatenate(ids, axis=0)


_PAIR_WIDTH = tuple(PEER_TOPK // (a + 1) for a in range(PEER_TOPK))


def _route_kernel(h_ref, wh_ref, wl_ref, g_ref, kh_ref, kl_ref, idx_ref, gate_ref):
    tm = h_ref.shape[0]
    hh, hl = _split_bf16(h_ref[...])
    q = _dot(hh, wh_ref[...]) + _dot(hh, wl_ref[...]) + _dot(hl, wh_ref[...])
    rowid = lax.broadcasted_iota(jnp.int32, (PEER_NKEYS, tm), 0).astype(F32)
    pos = lax.broadcasted_iota(jnp.int32, (sum(_PAIR_WIDTH), tm), 0).astype(F32)
    for p in range(PEER_HEADS):
        tops = []
        for half in range(2):
            seg = q[:, (2 * p + half) * PEER_HALF:(2 * p + half + 1) * PEER_HALF]
            ms = jnp.mean(seg * seg, axis=-1, keepdims=True)
            qh, ql = _split_bf16(seg * lax.rsqrt(ms + EPS) * g_ref[...])
            kh, kl = kh_ref[p, half], kl_ref[p, half]
            s_t = _dot_nt(kh, qh) + _dot_nt(kh, ql) + _dot_nt(kl, qh)
            tops.append(_topk_rows(s_t, rowid, PEER_TOPK))
        (s1, i1), (s2, i2) = tops
        cand = jnp.concatenate([s1[a:a + 1] + s2[:w] for a, w in enumerate(_PAIR_WIDTH)], axis=0)
        cidx = jnp.concatenate([i1[a:a + 1] * float(PEER_NKEYS) + i2[:w] for a, w in enumerate(_PAIR_WIDTH)],
                               axis=0)
        top, eid = _topk_rows(cand, pos, PEER_TOPK, payload=cidx)
        e = jnp.exp(top - top[0:1])
        rs = slice(p * PEER_TOPK, (p + 1) * PEER_TOPK)
        idx_ref[rs, :] = eid.astype(jnp.int32)
        gate_ref[rs, :] = e / e.sum(axis=0, keepdims=True)


def _peer_route(h2, p):
    T, D = h2.shape
    tm = min(T, 256)
    wspec = pl.BlockSpec((D, 2 * PEER_HEADS * PEER_HALF), lambda i: (0, 0))
    kspec = pl.BlockSpec((PEER_HEADS, 2, PEER_NKEYS, PEER_HALF), lambda i: (0, 0, 0, 0))
    return pl.pallas_call(
        _route_kernel,
        grid=(T // tm,),
        in_specs=[pl.BlockSpec((tm, D), lambda i: (i, 0)), wspec, wspec,
                  pl.BlockSpec((1, PEER_HALF), lambda i: (0, 0)), kspec, kspec],
        out_specs=[pl.BlockSpec((PEER_PICKS, tm), lambda i: (0, i)),
                   pl.BlockSpec((PEER_PICKS, tm), lambda i: (0, i))],
        out_shape=[jax.ShapeDtypeStruct((PEER_PICKS, T), jnp.int32),
                   jax.ShapeDtypeStruct((PEER_PICKS, T), F32)],
        compiler_params=_cparams(("arbitrary",), 48),
        name="peer_route",
    )(h2, p["w_pq_hi"], p["w_pq_lo"], p["pq_g"], p["sub_keys_hi"], p["sub_keys_lo"])


def _gather_kernel(idx_ref, gate_ref, h_ref, x_ref, g2_ref, uv_hbm, dep_hbm, o_ref, buf, sem):
    tm = h_ref.shape[0]
    nbuf = buf.shape[0]
    ahead = nbuf - 1

    def issue(t, slot):
        for k in range(PEER_PICKS):
            row = pl.multiple_of(idx_ref[t * PEER_PICKS + k] * UV_CHUNKS, UV_CHUNKS)
            pltpu.make_async_copy(uv_hbm.at[pl.ds(row, UV_CHUNKS)],
                                  buf.at[slot, pl.ds(k * UV_CHUNKS, UV_CHUNKS)], sem.at[slot]).start()

    def wait(slot):
        pltpu.make_async_copy(uv_hbm.at[pl.ds(0, PEER_PICKS * UV_CHUNKS)], buf.at[slot], sem.at[slot]).wait()

    for t0 in range(ahead):
        issue(t0, t0)

    lane = lax.broadcasted_iota(jnp.int32, (PEER_PICKS, tm), 1)
    half = UV_CHUNKS // 2

    def token(t, slot, h_row):
        wait(slot)
        acc = None
        for c in range(half):
            u_c = buf[slot, pl.ds(c, PEER_PICKS, stride=UV_CHUNKS), :]
            term = u_c * h_row[:, c * LANES:(c + 1) * LANES]
            acc = term if acc is None else acc + term
        d = acc.sum(axis=1, keepdims=True)
        act = 0.5 * d * (1.0 + lax.erf(d * (2.0 ** -0.5)))
        gcol = jnp.where(lane == t, gate_ref[...], 0.0).sum(axis=1, keepdims=True)
        w = gcol * act
        mix = []
        for c in range(half):
            v_c = buf[slot, pl.ds(half + c, PEER_PICKS, stride=UV_CHUNKS), :]
            mix.append((v_c * w).sum(axis=0, keepdims=True))
        return jnp.concatenate(mix, axis=1)

    def group(g, carry):
        t0 = pl.multiple_of(g * nbuf, nbuf)
        h_rows = h_ref[pl.ds(t0, nbuf), :]
        outs = []
        for j in range(nbuf):
            nxt = t0 + j + ahead

            @pl.when(nxt < tm)
            def _():
                issue(nxt, (j + ahead) % nbuf)

            outs.append(token(t0 + j, j, h_rows[j:j + 1]))
        o_ref[pl.ds(t0, nbuf), :] = x_ref[pl.ds(t0, nbuf), :] + g2_ref[0] * jnp.concatenate(outs, axis=0)
        return carry

    lax.fori_loop(0, tm // nbuf, group, 0)


def _peer_gather(idx_flat, gate_t, h2, x, g2, uv, tokens_per_batch, tok_off, n_tok, dep=None):
    D = h2.shape[1]
    tm = GATHER_TM
    off = tok_off // tm
    return pl.pallas_call(
        _gather_kernel,
        grid=(n_tok // tm,),
        in_specs=[pl.BlockSpec((tm * PEER_PICKS,), lambda i: (i + off,), memory_space=pltpu.SMEM),
                  pl.BlockSpec((PEER_PICKS, tm), lambda i: (0, i + off)),
                  pl.BlockSpec((tm, D), lambda i: (i + off, 0)),
                  pl.BlockSpec((tm, D), lambda i: (i + off, 0)),
                  pl.BlockSpec((1, 1, D), lambda i: ((tok_off + i * tm) // tokens_per_batch, 0, 0)),
                  pl.BlockSpec(memory_space=pl.ANY),
                  pl.BlockSpec(memory_space=pl.ANY)],
        out_specs=pl.BlockSpec((tm, D), lambda i: (i, 0)),
        out_shape=jax.ShapeDtypeStruct((n_tok, D), F32),
        scratch_shapes=[pltpu.VMEM((GATHER_NBUF, PEER_PICKS * UV_CHUNKS, LANES), F32),
                        pltpu.SemaphoreType.DMA((GATHER_NBUF,))],
        compiler_params=_cparams(("arbitrary",), 32),
        name="peer_gather",
    )(idx_flat, gate_t, h2, x, g2, uv, g2 if dep is None else dep)


SC_WORKERS = 32
SC_LANES = 16
SC_STREAM_PICKS = 32
SC_STREAMS = PEER_PICKS // SC_STREAM_PICKS
SC_INDEX_TOKENS = 128
SC_COL_UNROLL = 4


def _sc_params():
    cp = pltpu.CompilerParams()
    if "needs_layout_passes" in pltpu.CompilerParams.__dataclass_fields__:
        cp = dataclasses.replace(cp, needs_layout_passes=False)
    return cp


def _sc_token_loop(table_hbm, idx_hbm, idx_v, bufs, per, load_token, compute, store_token):
    wid = lax.axis_index("s") * 2 + lax.axis_index("c")
    base = wid * per
    sb = min(per, SC_INDEX_TOKENS)

    def start(i, ch, b):
        rows, sem = bufs[b]
        pltpu.async_copy(table_hbm.at[idx_v.at[pl.ds(i * PEER_PICKS + ch * SC_STREAM_PICKS, SC_STREAM_PICKS)]],
                         rows, sem)

    def wait(b):
        rows, sem = bufs[b]
        pltpu.make_async_copy(table_hbm.at[idx_v.at[pl.ds(0, SC_STREAM_PICKS)]], rows, sem).wait()

    @pl.loop(0, per // sb)
    def _(blk):
        tok0 = base + blk * sb
        pltpu.sync_copy(idx_hbm.at[pl.ds(tok0 * PEER_PICKS, sb * PEER_PICKS)], idx_v)
        start(0, 0, 0)
        start(0, 1, 1)

        @pl.loop(0, sb)
        def _(i):
            nxt = jnp.minimum(i + 1, sb - 1)
            load_token(tok0 + i)
            for ch in range(SC_STREAMS):
                b = ch % 2
                wait(b)
                compute(ch, bufs[b][0])
                if ch + 2 < SC_STREAMS:
                    start(i, ch + 2, b)
                else:
                    start(nxt, ch + 2 - SC_STREAMS, b)
            store_token(tok0 + i)

        wait(0)
        wait(1)


def _sc_scratch(per):
    sb = min(per, SC_INDEX_TOKENS)
    rows = pltpu.VMEM((SC_STREAM_PICKS, D_MODEL), F32)
    return [pltpu.VMEM((sb * PEER_PICKS,), jnp.int32), rows, rows, pltpu.SemaphoreType.DMA, pltpu.SemaphoreType.DMA]


def _sc_dots(u_tab, idx_flat, h_flat, tok_off, n_tok):
    per = n_tok // SC_WORKERS
    L = SC_LANES

    @functools.partial(
        pl.kernel, mesh=plsc.VectorSubcoreMesh(core_axis_name="c", subcore_axis_name="s"),
        out_type=jax.ShapeDtypeStruct((n_tok * PEER_PICKS,), F32),
        scratch_types=_sc_scratch(per) + [pltpu.VMEM((D_MODEL,), F32), pltpu.VMEM((PEER_PICKS,), F32)],
        compiler_params=_sc_params(),
    )
    def kern(u_hbm, idx_hbm, x_hbm, d_hbm, idx_v, rows0, rows1, sem0, sem1, x_v, d_v):
        lane = lax.iota(jnp.int32, L)

        def load_token(t):
            pltpu.sync_copy(x_hbm.at[tok_off + t], x_v)

        def compute(ch, rows):
            for g in range(SC_STREAM_PICKS // L):
                def cstep(cb, accs):
                    accs = list(accs)
                    for cc in range(SC_COL_UNROLL):
                        col = pl.ds((cb * SC_COL_UNROLL + cc) * L, L)
                        xv = x_v[col]
                        for j in range(L):
                            accs[j] = accs[j] + rows[g * L + j, col] * xv
                    return tuple(accs)

                accs = lax.fori_loop(0, D_MODEL // (L * SC_COL_UNROLL), cstep,
                                     tuple(jnp.zeros((L,), F32) for _ in range(L)))
                dvec = jnp.zeros((L,), F32)
                for j in range(L):
                    dvec = jnp.where(lane == j, jnp.sum(accs[j]), dvec)
                d_v[pl.ds(ch * SC_STREAM_PICKS + g * L, L)] = dvec

        def store_token(t):
            pltpu.sync_copy(d_v, d_hbm.at[pl.ds(t * PEER_PICKS, PEER_PICKS)])

        _sc_token_loop(u_hbm, idx_hbm, idx_v, ((rows0, sem0), (rows1, sem1)), per, load_token, compute,
                       store_token)

    return kern(u_tab, idx_flat, h_flat)


def _sc_mix(v_tab, idx_flat, w_flat, n_tok):
    per = n_tok // SC_WORKERS
    L = SC_LANES

    @functools.partial(
        pl.kernel, mesh=plsc.VectorSubcoreMesh(core_axis_name="c", subcore_axis_name="s"),
        out_type=jax.ShapeDtypeStruct((n_tok, D_MODEL), F32),
        scratch_types=_sc_scratch(per) + [pltpu.VMEM((PEER_PICKS,), F32), pltpu.VMEM((D_MODEL,), F32)],
        compiler_params=_sc_params(),
    )
    def kern(v_hbm, idx_hbm, w_hbm, o_hbm, idx_v, rows0, rows1, sem0, sem1, w_v, o_v):
        def load_token(t):
            pltpu.sync_copy(w_hbm.at[pl.ds(t * PEER_PICKS, PEER_PICKS)], w_v)

        def compute(ch, rows):
            for cg in range(D_MODEL // (L * L)):
                cols = [pl.ds((cg * L + c) * L, L) for c in range(L)]
                init = (tuple(jnp.zeros((L,), F32) for _ in cols) if ch == 0 else tuple(o_v[c] for c in cols))

                def kstep(k, accs):
                    wk = plsc.load_gather(w_v, [jnp.full((L,), ch * SC_STREAM_PICKS, jnp.int32) + k])
                    return tuple(acc + rows[k, c] * wk for acc, c in zip(accs, cols))

                accs = lax.fori_loop(0, SC_STREAM_PICKS, kstep, init)
                for acc, c in zip(accs, cols):
                    o_v[c] = acc

        def store_token(t):
            pltpu.sync_copy(o_v, o_hbm.at[t])

        _sc_token_loop(v_hbm, idx_hbm, idx_v, ((rows0, sem0), (rows1, sem1)), per, load_token, compute,
                       store_token)

    return kern(v_tab, idx_flat, w_flat)


def _act_kernel(d_ref, g_ref, dep_ref, w_ref):
    d = d_ref[...]
    w_ref[...] = g_ref[...] * (0.5 * d * (1.0 + lax.erf(d * (2.0 ** -0.5))))


def _peer_act(d, gate, dep):
    n = d.shape[0]
    tm = min(n, 2048)
    blk = pl.BlockSpec((tm, PEER_PICKS), lambda i: (i, 0))
    return pl.pallas_call(
        _act_kernel, grid=(n // tm,),
        in_specs=[blk, blk, pl.BlockSpec(memory_space=pl.ANY)], out_specs=blk,
        out_shape=jax.ShapeDtypeStruct(d.shape, F32),
        compiler_params=_cparams(("arbitrary",), 32), name="peer_act",
    )(d, gate, dep)


def _residual_kernel(x_ref, g2_ref, m_ref, o_ref):
    o_ref[...] = x_ref[...] + g2_ref[0] * m_ref[...]


def _peer_residual(x, g2, mix, tok_off, tokens_per_batch):
    n, D = mix.shape
    tm = min(n, 512)
    off = tok_off // tm
    return pl.pallas_call(
        _residual_kernel, grid=(n // tm,),
        in_specs=[pl.BlockSpec((tm, D), lambda i: (i + off, 0)),
                  pl.BlockSpec((1, 1, D), lambda i: ((tok_off + i * tm) // tokens_per_batch, 0, 0)),
                  pl.BlockSpec((tm, D), lambda i: (i, 0))],
        out_specs=pl.BlockSpec((tm, D), lambda i: (i, 0)),
        out_shape=jax.ShapeDtypeStruct((n, D), F32),
        compiler_params=_cparams(("arbitrary",), 32), name="peer_residual",
    )(x, g2, mix)


def _sc_share(n_tok):
    unit = 2 * SC_WORKERS * SC_INDEX_TOKENS
    return n_tok // 2 if n_tok % unit == 0 else 0


def _peer(h2, x, g2, p, sc_tokens=0):
    B, T, D = x.shape
    n = B * T
    hf = h2.reshape(n, D)
    xf = x.reshape(n, D)
    idx_t, gate_t = _peer_route(hf, p)
    n_tc = n - sc_tokens
    if sc_tokens == 0:
        out = _peer_gather(idx_t.T.reshape(-1), gate_t, hf, xf, g2, p["uv"], T, 0, n)
        return out.reshape(B, T, D)
    idx_sc = idx_t[:, n_tc:].T.reshape(-1)
    gate_sc = gate_t[:, n_tc:].T
    idx_tc = idx_t[:, :n_tc].T.reshape(-1)
    half = n_tc // 2
    d = _sc_dots(p["peer_u"], idx_sc, hf, n_tc, sc_tokens)
    out_a = _peer_gather(idx_tc, gate_t, hf, xf, g2, p["uv"], T, 0, half)
    w = _peer_act(d.reshape(sc_tokens, PEER_PICKS), gate_sc, out_a)
    mix = _sc_mix(p["peer_v"], idx_sc, w.reshape(-1), sc_tokens)
    out_b = _peer_gather(idx_tc, gate_t, hf, xf, g2, p["uv"], T, half, n_tc - half, dep=w)
    out_sc = _peer_residual(xf, g2, mix, n_tc, T)
    return jnp.concatenate([out_a, out_b, out_sc], axis=0).reshape(B, T, D)


def _rearrange_w_in(w):
    cols = [w[:, o:o + n] for _, o, n, _ in _SEGS]
    cols.append(jnp.zeros((w.shape[0], PROJ_W - sum(n for _, _, n, _ in _SEGS)), w.dtype))
    return jnp.concatenate(cols, axis=1).astype(BF16)


def _rope_tables(S):
    t = jnp.arange(S)
    row = (t // GRID_W).astype(F32)[:, None]
    colp = (t % GRID_W).astype(F32)[:, None]
    nf = HEAD_DIM // 4
    inv = ROPE_THETA ** (-jnp.arange(nf, dtype=F32) / nf)
    ang = jnp.concatenate([row * inv, row * inv, colp * inv, colp * inv], axis=-1)
    ang = jnp.concatenate([ang, ang], axis=-1)
    return jnp.cos(ang), jnp.sin(ang)


def _layer_params(i, a):
    two = lambda g: jnp.concatenate([g, g]).reshape(1, LANES)
    hid = np.arange(LANES) // HEAD_DIM
    E = PEER_EXPERTS
    w_pq_hi, w_pq_lo = _split_bf16(a["w_pq"][i])
    keys_hi, keys_lo = _split_bf16(a["sub_keys"][i])
    return {
        "w_mod": a["w_mod"][i], "b_mod": a["b_mod"][i],
        "norm1": a["norm1"][i], "norm2": a["norm2"][i].reshape(1, D_MODEL),
        "w_in": _rearrange_w_in(a["w_in"][i]),
        "qn_a2": two(a["qn_a"][i]), "kn_a2": two(a["kn_a"][i]),
        "qn_b2": two(a["qn_b"][i]), "kn_b2": two(a["kn_b"][i]),
        "m128": jnp.asarray(hid[:, None] == hid[None, :], BF16),
        "sink_a": a["sink_a"][i], "rpb_b": a["rpb_b"][i],
        "wa2_f": a["wa2_f"][i], "wa2_b": a["wa2_b"][i],
        "ba_f": a["ba_f"][i].reshape(1, -1), "ba_b": a["ba_b"][i].reshape(1, -1),
        "gn_c": a["gn_c"][i].reshape(1, C_DV),
        "w_oa": a["w_oa"][i].astype(BF16), "w_ob": a["w_ob"][i].astype(BF16),
        "w_oc": a["w_oc"][i].astype(BF16), "w_out": a["w_out"][i].astype(BF16),
        "w_pq_hi": w_pq_hi, "w_pq_lo": w_pq_lo, "pq_g": a["pq_g"][i].reshape(1, PEER_HALF),
        "sub_keys_hi": keys_hi, "sub_keys_lo": keys_lo,
        "peer_u": a["peer_u"][i], "peer_v": a["peer_v"][i],
        "uv": jnp.concatenate([a["peer_u"][i].reshape(E, UV_CHUNKS // 2, LANES),
                               a["peer_v"][i].reshape(E, UV_CHUNKS // 2, LANES)],
                              axis=1).reshape(E * UV_CHUNKS, LANES),
    }


def _layer(x, ctx, c_all, p, cos, sin, last):
    B, S, D = x.shape
    L = ctx.shape[1]
    ones = jnp.ones((L, LANES), F32)
    zeros = jnp.zeros((L, LANES), F32)
    state0 = jnp.zeros((B, C_HEADS, C_DV, C_DK), F32)
    mods = _modulation(c_all, p["w_mod"], p["b_mod"])
    lat = [mods[:B, k * D:(k + 1) * D].reshape(B, 1, D) for k in range(N_MOD)]
    cm = [jnp.broadcast_to(mods[B, k * D:(k + 1) * D].reshape(1, 1, D), (B, 1, D)) for k in range(N_MOD)]
    sh1, sc1, g1, sh2, sc2, g2 = lat
    csh1, csc1, cg1, csh2, csc2, cg2 = cm

    proj = _norm_proj(x, p["norm1"], sh1, sc1, p["w_in"])
    cproj = _norm_proj(ctx, p["norm1"], csh1, csc1, p["w_in"])
    qa, ka, qb, kb = _qk_prep(proj, cos, sin, p)
    cqa, cka, cqb, ckb = _qk_prep(cproj, ones, zeros, p)

    o_a = _attn_a(qa, ka, proj, cka, cproj, p["sink_a"])
    o_b = _attn_b(qb, kb, proj, ckb, cproj, _na_bias(p["rpb_b"], S // GRID_W))
    co_f, co_b, s_f, s_b = _gla(cproj, p, state0, state0)
    o_f, o_bk, _, _ = _gla(proj, p, s_f, s_b)
    x, h2 = _merge(o_a, o_b, o_f, o_bk, proj, x, g1, sh2, sc2, p)
    x = _peer(h2, x, g2, p, sc_tokens=_sc_share(B * S))
    if not last:
        co_a = _ctx_attn(cqa, cka, cproj, "vA", p["sink_a"], A_KV_HEADS, True)
        co_bb = _ctx_attn(cqb, ckb, cproj, "vB", p["sink_a"], B_HEADS, False)
        ctx, hc2 = _merge(co_a, co_bb, co_f, co_b, cproj, ctx, cg1, csh2, csc2, p)
        ctx = _peer(hc2, ctx, cg2, p)
    return x, ctx


def _forward(a):
    x, ctx = a["x"], a["ctx"]
    B, S, D = x.shape
    cos, sin = _rope_tables(S)
    c_all = jnp.zeros((8, D), F32).at[:B].set(a["c"]).at[B].set(a["c_ctx"])
    for i in range(DEPTH):
        x, ctx = _layer(x, ctx, c_all, _layer_params(i, a), cos, sin, i == DEPTH - 1)
    return x


def kernel(x, c, ctx, c_ctx, w_mod, b_mod, norm1, norm2, w_in, qn_a, kn_a, sink_a, qn_b, kn_b, rpb_b,
           wa2_f, ba_f, wa2_b, ba_b, gn_c, w_oa, w_ob, w_oc, w_out, w_pq, pq_g, sub_keys, peer_u, peer_v):
    return _forward(dict(
        x=x, c=c, ctx=ctx, c_ctx=c_ctx, w_mod=w_mod, b_mod=b_mod, norm1=norm1, norm2=norm2, w_in=w_in,
        qn_a=qn_a, kn_a=kn_a, sink_a=sink_a, qn_b=qn_b, kn_b=kn_b, rpb_b=rpb_b, wa2_f=wa2_f, ba_f=ba_f,
        wa2_b=wa2_b, ba_b=ba_b, gn_c=gn_c, w_oa=w_oa, w_ob=w_ob, w_oc=w_oc, w_out=w_out, w_pq=w_pq,
        pq_g=pq_g, sub_keys=sub_keys, peer_u=peer_u, peer_v=peer_v))
```

```python
import dataclasses
import functools

import numpy as np
import jax
import jax.numpy as jnp
from jax import lax
from jax.experimental import pallas as pl
from jax.experimental.pallas import tpu as pltpu
from jax.experimental.pallas import tpu_sc as plsc

F32 = jnp.float32
BF16 = jnp.bfloat16

D_MODEL = 1024
DEPTH = 4
GRID_W = 64
HEAD_DIM = 64
ROPE_THETA = 10000.0
EPS = 1e-6
NEG = -1e30
N_MOD = 6
A_HEADS = 8
A_KV_HEADS = 2
A_BLOCK = 128
B_HEADS = 8
NA_ROWS = 8
NA_COLS = 16
C_HEADS = 4
C_DK = 128
C_DV = 256
C_GATE_RANK = 16
C_GATE_TAU = 16.0
C_CHUNK = 64
PEER_HEADS = 8
PEER_NKEYS = 128
PEER_EXPERTS = PEER_NKEYS * PEER_NKEYS
PEER_HALF = 128
PEER_TOPK = 16
PEER_PICKS = PEER_HEADS * PEER_TOPK

LANES = 128
MIB = 1024 * 1024

_SEGS = (
    ("vC", 1792, 1024, 0), ("rC", 4384, 1024, 1024), ("gA", 5408, 1024, 2048),
    ("gB", 6432, 1024, 3072), ("gC", 7456, 1024, 4096), ("kB", 256, 512, 5120),
    ("vB", 768, 512, 5632), ("kC", 1280, 512, 6144), ("qA", 2848, 512, 6656),
    ("qB", 3360, 512, 7168), ("qC", 3872, 512, 7680), ("kA", 0, 128, 8192),
    ("vA", 128, 128, 8320), ("aC", 2816, 32, 8448),
)
PROJ_W = 8704
PROJ_TN = 2176
COLBLK = {name: new // (LANES if w < LANES else w) for name, _, w, new in _SEGS}
PROJ_DT = BF16

NA_TR = 4
GATHER_TM = 128
GATHER_NBUF = 8
UV_CHUNKS = 16


def _cparams(sem, vmem_mib):
    return pltpu.CompilerParams(dimension_semantics=sem, vmem_limit_bytes=vmem_mib * MIB)


def _split_bf16(a):
    hi = a.astype(BF16)
    lo = (a - hi.astype(F32)).astype(BF16)
    return hi, lo


def _dot(a, b):
    return jnp.dot(a, b, preferred_element_type=F32)


def _dot_nt(a, b):
    return lax.dot_general(a, b, (((1,), (1,)), ((), ())), preferred_element_type=F32)


def _dot_tn(a, b):
    return lax.dot_general(a, b, (((0,), (0,)), ((), ())), preferred_element_type=F32)


def _dot_precise(a, b):
    ah, al = _split_bf16(a)
    bh, bl = _split_bf16(b)
    return _dot(ah, bh) + _dot(ah, bl) + _dot(al, bh)


def _mod_kernel(c_ref, w_ref, b_ref, o_ref):
    c = c_ref[...]
    act = c * jax.nn.sigmoid(c)
    o_ref[...] = _dot_precise(act, w_ref[...]) + b_ref[...]


def _modulation(c_all, w_mod, b_mod):
    n = w_mod.shape[1]
    tn = D_MODEL
    return pl.pallas_call(
        _mod_kernel,
        grid=(n // tn,),
        in_specs=[pl.BlockSpec((8, D_MODEL), lambda j: (0, 0)),
                  pl.BlockSpec((D_MODEL, tn), lambda j: (0, j)),
                  pl.BlockSpec((1, tn), lambda j: (0, j))],
        out_specs=pl.BlockSpec((8, tn), lambda j: (0, j)),
        out_shape=jax.ShapeDtypeStruct((8, n), F32),
        compiler_params=_cparams(("arbitrary",), 32),
        name="modulation",
    )(c_all, w_mod, b_mod.reshape(1, n))


def _modulated_norm(x, g, sh, sc):
    ms = jnp.mean(x * x, axis=-1, keepdims=True)
    return (x * lax.rsqrt(ms + EPS) * g) * (1.0 + sc) + sh


def _norm_proj_kernel(x_ref, g_ref, sh_ref, sc_ref, w_ref, o_ref):
    h = _modulated_norm(x_ref[0], g_ref[...], sh_ref[0], sc_ref[0])
    o_ref[0] = _dot(h.astype(BF16), w_ref[...]).astype(o_ref.dtype)


def _norm_proj(x, g, sh, sc, w):
    B, T, D = x.shape
    tm = min(T, 512)
    return pl.pallas_call(
        _norm_proj_kernel,
        grid=(PROJ_W // PROJ_TN, B, T // tm),
        in_specs=[pl.BlockSpec((1, tm, D), lambda j, b, i: (b, i, 0)),
                  pl.BlockSpec((1, D), lambda j, b, i: (0, 0)),
                  pl.BlockSpec((1, 1, D), lambda j, b, i: (b, 0, 0)),
                  pl.BlockSpec((1, 1, D), lambda j, b, i: (b, 0, 0)),
                  pl.BlockSpec((D, PROJ_TN), lambda j, b, i: (0, j))],
        out_specs=pl.BlockSpec((1, tm, PROJ_TN), lambda j, b, i: (b, i, j)),
        out_shape=jax.ShapeDtypeStruct((B, T, PROJ_W), PROJ_DT),
        compiler_params=_cparams(("arbitrary", "arbitrary", "arbitrary"), 48),
        name="norm_proj",
    )(x, g.reshape(1, D), sh, sc, w)


def _head_norm(z, g2, m128):
    hi, lo = _split_bf16(z * z)
    ss = _dot(hi, m128) + _dot(lo, m128)
    return z * lax.rsqrt(ss * (1.0 / HEAD_DIM) + EPS) * g2


def _rope128(z, cos, sin, even16):
    rot = jnp.where(even16, -pltpu.roll(z, LANES - 16, 1), pltpu.roll(z, 16, 1))
    return z * cos + rot * sin


def _prep_kernel(qa_ref, qb_ref, kb_ref, ka_ref, cos_ref, sin_ref, gqa_ref, gka_ref, gqb_ref,
                 gkb_ref, m_ref, oqa_ref, oka_ref, oqb_ref, okb_ref):
    m128 = m_ref[...]
    cos = cos_ref[...]
    sin = sin_ref[...]
    lane = lax.broadcasted_iota(jnp.int32, cos.shape, 1)
    even16 = ((lane // 16) % 2) == 0
    scale = HEAD_DIM ** -0.5
    for s in range(4):
        sl = slice(s * LANES, (s + 1) * LANES)
        za = _head_norm(qa_ref[0][:, sl].astype(F32), gqa_ref[...], m128)
        oqa_ref[0, :, sl] = (_rope128(za, cos, sin, even16) * scale).astype(BF16)
        zb = _head_norm(qb_ref[0][:, sl].astype(F32), gqb_ref[...], m128)
        oqb_ref[0, :, sl] = (zb * scale).astype(BF16)
        zk = _head_norm(kb_ref[0][:, sl].astype(F32), gkb_ref[...], m128)
        okb_ref[0, :, sl] = zk.astype(BF16)
    zk = _head_norm(ka_ref[0].astype(F32), gka_ref[...], m128)
    oka_ref[0] = _rope128(zk, cos, sin, even16).astype(BF16)


def _qk_prep(proj, cos, sin, p):
    B, T, _ = proj.shape
    tm = min(T, 256)

    def col(name, w):
        blk = COLBLK[name]
        return pl.BlockSpec((1, tm, w), lambda b, i: (b, i, blk))

    vec = pl.BlockSpec((1, LANES), lambda b, i: (0, 0))
    tab = pl.BlockSpec((tm, LANES), lambda b, i: (i, 0))
    out = lambda w: pl.BlockSpec((1, tm, w), lambda b, i: (b, i, 0))
    return pl.pallas_call(
        _prep_kernel,
        grid=(B, T // tm),
        in_specs=[col("qA", 512), col("qB", 512), col("kB", 512), col("kA", 128), tab, tab,
                  vec, vec, vec, vec, pl.BlockSpec((LANES, LANES), lambda b, i: (0, 0))],
        out_specs=[out(512), out(128), out(512), out(512)],
        out_shape=[jax.ShapeDtypeStruct((B, T, 512), BF16), jax.ShapeDtypeStruct((B, T, 128), BF16),
                   jax.ShapeDtypeStruct((B, T, 512), BF16), jax.ShapeDtypeStruct((B, T, 512), BF16)],
        compiler_params=_cparams(("arbitrary", "arbitrary"), 32),
        name="qk_prep",
    )(proj, proj, proj, proj, cos, sin, p["qn_a2"], p["kn_a2"], p["qn_b2"], p["kn_b2"], p["m128"])


def _softmax_pv(parts, sink):
    m = parts[0][0].max(axis=-1, keepdims=True)
    for s, _ in parts[1:]:
        m = jnp.maximum(m, s.max(axis=-1, keepdims=True))
    if sink is not None:
        m = jnp.maximum(m, sink)
    den = None
    acc = None
    for s, v in parts:
        e = jnp.exp(s - m)
        ds = e.sum(axis=-1, keepdims=True)
        den = ds if den is None else den + ds
        pv = _dot(e.astype(BF16), v)
        acc = pv if acc is None else acc + pv
    if sink is not None:
        den = den + jnp.exp(sink - m)
    return acc / den


def _attn_a_kernel(sink_ref, q_ref, kp_ref, ko_ref, kn_ref, vp_ref, vo_ref, vn_ref, ck_ref, cv_ref, o_ref):
    n = pl.program_id(1)
    nb = pl.num_programs(1)
    q = q_ref[0]
    k_loc = jnp.concatenate([kp_ref[0], ko_ref[0], kn_ref[0]], axis=0)
    v_loc = jnp.concatenate([vp_ref[0], vo_ref[0], vn_ref[0]], axis=0).astype(BF16)
    ck = ck_ref[0]
    cv = cv_ref[0].astype(BF16)
    qi = lax.broadcasted_iota(jnp.int32, (A_BLOCK, 3 * A_BLOCK), 0)
    kc = lax.broadcasted_iota(jnp.int32, (A_BLOCK, 3 * A_BLOCK), 1)
    rel = kc - qi
    valid = (rel >= 0) & (rel <= 2 * A_BLOCK)
    valid = valid & ((n > 0) | (kc >= A_BLOCK)) & ((n < nb - 1) | (kc < 2 * A_BLOCK))
    rep = A_HEADS // A_KV_HEADS
    for h in range(A_HEADS):
        g = h // rep
        qh = q[:, h * HEAD_DIM:(h + 1) * HEAD_DIM]
        gs = slice(g * HEAD_DIM, (g + 1) * HEAD_DIM)
        s_loc = jnp.where(valid, _dot_nt(qh, k_loc[:, gs]), NEG)
        s_ctx = _dot_nt(qh, ck[:, gs])
        o = _softmax_pv([(s_loc, v_loc[:, gs]), (s_ctx, cv[:, gs])], sink_ref[h])
        o_ref[0, :, h * HEAD_DIM:(h + 1) * HEAD_DIM] = o.astype(o_ref.dtype)


def _attn_a(qa, ka, proj, cka, cproj, sink):
    B, S, _ = qa.shape
    L = cka.shape[1]
    nb = S // A_BLOCK
    vblk = COLBLK["vA"]
    blk = lambda f: pl.BlockSpec((1, A_BLOCK, 128), f)
    return pl.pallas_call(
        _attn_a_kernel,
        grid=(B, nb),
        in_specs=[pl.BlockSpec(memory_space=pltpu.SMEM),
                  pl.BlockSpec((1, A_BLOCK, 512), lambda b, n: (b, n, 0)),
                  blk(lambda b, n: (b, jnp.maximum(n - 1, 0), 0)),
                  blk(lambda b, n: (b, n, 0)),
                  blk(lambda b, n: (b, jnp.minimum(n + 1, nb - 1), 0)),
                  blk(lambda b, n: (b, jnp.maximum(n - 1, 0), vblk)),
                  blk(lambda b, n: (b, n, vblk)),
                  blk(lambda b, n: (b, jnp.minimum(n + 1, nb - 1), vblk)),
                  pl.BlockSpec((1, L, 128), lambda b, n: (b, 0, 0)),
                  pl.BlockSpec((1, L, 128), lambda b, n: (b, 0, vblk))],
        out_specs=pl.BlockSpec((1, A_BLOCK, 512), lambda b, n: (b, n, 0)),
        out_shape=jax.ShapeDtypeStruct((B, S, 512), BF16),
        compiler_params=_cparams(("arbitrary", "arbitrary"), 32),
        name="attn_window",
    )(sink, qa, ka, ka, ka, proj, proj, proj, cka, cproj)


def _attn_b_kernel(q_ref, k0_ref, k1_ref, k2_ref, v0_ref, v1_ref, v2_ref, ck_ref, cv_ref, bias_ref, o_ref):
    q = q_ref[0]
    k_loc = jnp.concatenate([k0_ref[0], k1_ref[0], k2_ref[0]], axis=0)
    v_loc = jnp.concatenate([v0_ref[0], v1_ref[0], v2_ref[0]], axis=0).astype(BF16)
    ck = ck_ref[0]
    cv = cv_ref[0].astype(BF16)
    for h in range(B_HEADS):
        hs = slice(h * HEAD_DIM, (h + 1) * HEAD_DIM)
        qh = q[:, hs]
        s_loc = _dot_nt(qh, k_loc[:, hs]) + bias_ref[0, h]
        s_ctx = _dot_nt(qh, ck[:, hs])
        o = _softmax_pv([(s_loc, v_loc[:, hs]), (s_ctx, cv[:, hs])], None)
        o_ref[0, :, hs] = o.astype(o_ref.dtype)


def _na_bias(rpb, rows):
    c = np.arange(GRID_W)
    col_start = np.clip(c - NA_COLS // 2, 0, GRID_W - NA_COLS)
    col_valid = (c[None, :] >= col_start[:, None]) & (c[None, :] < col_start[:, None] + NA_COLS)
    dc = np.clip(c[None, :] - c[:, None], -(NA_COLS - 1), NA_COLS - 1) + (NA_COLS - 1)
    toe = jnp.where(jnp.asarray(col_valid)[None, None], jnp.take(rpb, jnp.asarray(dc.reshape(-1)), axis=2)
                    .reshape(B_HEADS, 2 * NA_ROWS - 1, GRID_W, GRID_W), NEG)
    masked = jnp.full((B_HEADS, GRID_W, GRID_W), NEG, F32)
    tiles = []
    for r0 in (0, NA_TR, rows - NA_TR):
        band = []
        for rq in range(r0, r0 + NA_TR):
            start = min(max(rq - NA_ROWS // 2, 0), rows - NA_ROWS)
            blocks = [toe[:, kr - rq + NA_ROWS - 1] if start <= kr < start + NA_ROWS else masked
                      for kr in range(r0 - NA_TR, r0 + 2 * NA_TR)]
            band.append(jnp.concatenate(blocks, axis=2))
        tiles.append(jnp.concatenate(band, axis=1))
    return jnp.stack(tiles).astype(F32)


def _attn_b(qb, kb, proj, ckb, cproj, bias):
    B, S, _ = qb.shape
    L = ckb.shape[1]
    tq = NA_TR * GRID_W
    ns = S // tq
    vblk = COLBLK["vB"]
    blk = lambda f: pl.BlockSpec((1, tq, 512), f)
    lo = lambda b, i: (b, jnp.maximum(i - 1, 0), 0)
    hi = lambda b, i: (b, jnp.minimum(i + 1, ns - 1), 0)
    variant = lambda b, i: (jnp.where(i == 0, 0, jnp.where(i == ns - 1, 2, 1)), 0, 0, 0)
    return pl.pallas_call(
        _attn_b_kernel,
        grid=(B, ns),
        in_specs=[blk(lambda b, i: (b, i, 0)),
                  blk(lo), blk(lambda b, i: (b, i, 0)), blk(hi),
                  blk(lambda b, i: lo(b, i)[:2] + (vblk,)),
                  blk(lambda b, i: (b, i, vblk)),
                  blk(lambda b, i: hi(b, i)[:2] + (vblk,)),
                  pl.BlockSpec((1, L, 512), lambda b, i: (b, 0, 0)),
                  pl.BlockSpec((1, L, 512), lambda b, i: (b, 0, vblk)),
                  pl.BlockSpec((1, B_HEADS, tq, 3 * tq), variant)],
        out_specs=blk(lambda b, i: (b, i, 0)),
        out_shape=jax.ShapeDtypeStruct((B, S, 512), BF16),
        compiler_params=_cparams(("arbitrary", "arbitrary"), 56),
        name="attn_neighbourhood",
    )(qb, kb, kb, kb, proj, proj, proj, ckb, cproj, bias)


def _ctx_attn_kernel(sink_ref, q_ref, k_ref, v_ref, o_ref, *, kv_heads, use_sink):
    q = q_ref[0]
    k = k_ref[0]
    v = v_ref[0].astype(BF16)
    rep = A_HEADS // kv_heads
    for h in range(A_HEADS):
        g = h // rep
        gs = slice(g * HEAD_DIM, (g + 1) * HEAD_DIM)
        s = _dot_nt(q[:, h * HEAD_DIM:(h + 1) * HEAD_DIM], k[:, gs])
        o = _softmax_pv([(s, v[:, gs])], sink_ref[h] if use_sink else None)
        o_ref[0, :, h * HEAD_DIM:(h + 1) * HEAD_DIM] = o.astype(o_ref.dtype)


def _ctx_attn(q, k, cproj, vname, sink, kv_heads, use_sink):
    B, L, _ = q.shape
    kw = kv_heads * HEAD_DIM
    vblk = COLBLK[vname]
    return pl.pallas_call(
        functools.partial(_ctx_attn_kernel, kv_heads=kv_heads, use_sink=use_sink),
        grid=(B,),
        in_specs=[pl.BlockSpec(memory_space=pltpu.SMEM),
                  pl.BlockSpec((1, L, 512), lambda b: (b, 0, 0)),
                  pl.BlockSpec((1, L, kw), lambda b: (b, 0, 0)),
                  pl.BlockSpec((1, L, kw), lambda b: (b, 0, vblk))],
        out_specs=pl.BlockSpec((1, L, 512), lambda b: (b, 0, 0)),
        out_shape=jax.ShapeDtypeStruct((B, L, 512), BF16),
        compiler_params=_cparams(("arbitrary",), 32),
        name="attn_context",
    )(sink, q, k, cproj)


def _gla_direction(q_ref, k_ref, v_ref, a_ref, w2_ref, b_ref, o_ref, S, forward):
    C = q_ref.shape[1]
    lo = 0 if forward else C_GATE_RANK
    a = a_ref[0][:, lo:lo + C_GATE_RANK].astype(F32)
    z = _dot_precise(a, w2_ref[...]) + b_ref[...]
    la = (jnp.minimum(z, 0.0) - jnp.log1p(jnp.exp(-jnp.abs(z)))) * (1.0 / C_GATE_TAU)
    ri = lax.broadcasted_iota(jnp.int32, (C, C), 0)
    ci = lax.broadcasted_iota(jnp.int32, (C, C), 1)
    keep = (ci <= ri) if forward else (ci >= ri)
    tri = keep.astype(BF16)
    la_hi, la_lo = _split_bf16(la)
    cum = _dot(tri, la_hi) + _dot(tri, la_lo)
    ref_row = C // 2 - 1 if forward else C // 2
    end_row = C - 1 if forward else 0
    scale = C_DK ** -0.5
    for h in range(C_HEADS):
        ks = slice(h * C_DK, (h + 1) * C_DK)
        vs = slice(h * C_DV, (h + 1) * C_DV)
        c = cum[:, ks]
        c_ref = c[ref_row:ref_row + 1]
        c_end = c[end_row:end_row + 1]
        q = q_ref[0][:, ks].astype(F32) * scale
        k = k_ref[0][:, ks].astype(F32)
        v = v_ref[0][:, vs].astype(BF16)
        att = _dot_nt((q * jnp.exp(c - c_ref)).astype(BF16), (k * jnp.exp(c_ref - c)).astype(BF16))
        att = jnp.where(keep, att, 0.0)
        st = S[h]
        o = _dot(att.astype(BF16), v) + _dot_nt((q * jnp.exp(c)).astype(BF16), st.astype(BF16))
        kd = (k * jnp.exp(c_end - c)).astype(BF16)
        S[h] = st * jnp.exp(c_end) + _dot_tn(v, kd)
        o_ref[0, :, vs] = o


def _gla_kernel(qf, kf, vf, af, qb, kb, vb, ab, w2f, w2b, bf, bb, s0f, s0b, of, ob, sff, sfb, Sf, Sb):
    j = pl.program_id(1)

    @pl.when(j == 0)
    def _():
        Sf[...] = s0f[0]
        Sb[...] = s0b[0]

    _gla_direction(qf, kf, vf, af, w2f, bf, of, Sf, True)
    _gla_direction(qb, kb, vb, ab, w2b, bb, ob, Sb, False)

    @pl.when(j == pl.num_programs(1) - 1)
    def _():
        sff[0] = Sf[...]
        sfb[0] = Sb[...]


def _gla(proj, p, s0f, s0b):
    B, T, _ = proj.shape
    C = C_CHUNK
    n = T // C
    fwd = lambda b, j: (b, j)
    bwd = lambda b, j: (b, n - 1 - j)

    def cols(order):
        mk = lambda name, w: pl.BlockSpec((1, C, w), lambda b, j: order(b, j) + (COLBLK[name],))
        return [mk("qC", 512), mk("kC", 512), mk("vC", 1024), mk("aC", 128)]

    wspec = pl.BlockSpec((C_GATE_RANK, 512), lambda b, j: (0, 0))
    bspec = pl.BlockSpec((1, 512), lambda b, j: (0, 0))
    sspec = pl.BlockSpec((1, C_HEADS, C_DV, C_DK), lambda b, j: (b, 0, 0, 0))
    st = jax.ShapeDtypeStruct((B, C_HEADS, C_DV, C_DK), F32)
    return pl.pallas_call(
        _gla_kernel,
        grid=(B, n),
        in_specs=cols(fwd) + cols(bwd) + [wspec, wspec, bspec, bspec, sspec, sspec],
        out_specs=[pl.BlockSpec((1, C, 1024), lambda b, j: (b, j, 0)),
                   pl.BlockSpec((1, C, 1024), lambda b, j: (b, n - 1 - j, 0)),
                   sspec, sspec],
        out_shape=[jax.ShapeDtypeStruct((B, T, 1024), F32), jax.ShapeDtypeStruct((B, T, 1024), F32), st, st],
        scratch_shapes=[pltpu.VMEM((C_HEADS, C_DV, C_DK), F32), pltpu.VMEM((C_HEADS, C_DV, C_DK), F32)],
        compiler_params=_cparams(("arbitrary", "arbitrary"), 32),
        name="gla_scan",
    )(proj, proj, proj, proj, proj, proj, proj, proj, p["wa2_f"], p["wa2_b"], p["ba_f"], p["ba_b"], s0f, s0b)


def _merge_kernel(oa_ref, ob_ref, of_ref, obk_ref, r_ref, ga_ref, gb_ref, gc_ref, x_ref, g1_ref, sh_ref,
                  sc_ref, n2_ref, gn_ref, woa_ref, wob_ref, woc_ref, wout_ref, xo_ref, h2_ref):
    o = of_ref[0] + obk_ref[0]
    r = r_ref[0].astype(F32)
    parts = []
    for h in range(C_HEADS):
        vs = slice(h * C_DV, (h + 1) * C_DV)
        oh = o[:, vs]
        ms = jnp.mean(oh * oh, axis=-1, keepdims=True)
        rh = r[:, vs]
        parts.append((oh * lax.rsqrt(ms + EPS) * gn_ref[...]) * (rh * jax.nn.sigmoid(rh)))
    oc = jnp.concatenate(parts, axis=-1).astype(BF16)
    y = (jax.nn.sigmoid(ga_ref[0].astype(F32)) * _dot(oa_ref[0], woa_ref[...])
         + jax.nn.sigmoid(gb_ref[0].astype(F32)) * _dot(ob_ref[0], wob_ref[...])
         + jax.nn.sigmoid(gc_ref[0].astype(F32)) * _dot(oc, woc_ref[...]))
    xn = x_ref[0] + g1_ref[0] * _dot(y.astype(BF16), wout_ref[...])
    xo_ref[0] = xn
    h2_ref[0] = _modulated_norm(xn, n2_ref[...], sh_ref[0], sc_ref[0])


def _merge(oa, ob, of, obk, proj, x, g1, sh2, sc2, p):
    B, T, D = x.shape
    tm = min(T, 256)
    tok = lambda w: pl.BlockSpec((1, tm, w), lambda b, i: (b, i, 0))
    col = lambda name: pl.BlockSpec((1, tm, 1024), lambda b, i: (b, i, COLBLK[name]))
    mod = pl.BlockSpec((1, 1, D), lambda b, i: (b, 0, 0))
    full = lambda r, c: pl.BlockSpec((r, c), lambda b, i: (0, 0))
    return pl.pallas_call(
        _merge_kernel,
        grid=(B, T // tm),
        in_specs=[tok(512), tok(512), tok(1024), tok(1024), col("rC"), col("gA"), col("gB"), col("gC"),
                  tok(D), mod, mod, mod, full(1, D), full(1, C_DV),
                  full(512, D), full(512, D), full(1024, D), full(D, D)],
        out_specs=[tok(D), tok(D)],
        out_shape=[jax.ShapeDtypeStruct((B, T, D), F32), jax.ShapeDtypeStruct((B, T, D), F32)],
        compiler_params=_cparams(("arbitrary", "arbitrary"), 56),
        name="merge",
    )(oa, ob, of, obk, proj, proj, proj, proj, x, g1, sh2, sc2, p["norm2"], p["gn_c"],
      p["w_oa"], p["w_ob"], p["w_oc"], p["w_out"])


def _topk_rows(arr, rowid, k, payload=None):
    n = arr.shape[0]
    vals, ids = [], []
    for _ in range(k):
        m = arr.max(axis=0, keepdims=True)
        sel = jnp.where(arr == m, rowid, float(n)).min(axis=0, keepdims=True)
        hit = rowid == sel
        vals.append(m)
        ids.append(sel if payload is None else jnp.where(hit, payload, -1.0).max(axis=0, keepdims=True))
        arr = jnp.where(hit, -jnp.inf, arr)
    return jnp.concatenate(vals, axis=0), jnp.concatenate(ids, axis=0)


_PAIR_WIDTH = tuple(PEER_TOPK // (a + 1) for a in range(PEER_TOPK))


def _route_kernel(h_ref, wh_ref, wl_ref, g_ref, kh_ref, kl_ref, idx_ref, gate_ref):
    tm = h_ref.shape[0]
    hh, hl = _split_bf16(h_ref[...])
    q = _dot(hh, wh_ref[...]) + _dot(hh, wl_ref[...]) + _dot(hl, wh_ref[...])
    rowid = lax.broadcasted_iota(jnp.int32, (PEER_NKEYS, tm), 0).astype(F32)
    pos = lax.broadcasted_iota(jnp.int32, (sum(_PAIR_WIDTH), tm), 0).astype(F32)
    for p in range(PEER_HEADS):
        tops = []
        for half in range(2):
            seg = q[:, (2 * p + half) * PEER_HALF:(2 * p + half + 1) * PEER_HALF]
            ms = jnp.mean(seg * seg, axis=-1, keepdims=True)
            qh, ql = _split_bf16(seg * lax.rsqrt(ms + EPS) * g_ref[...])
            kh, kl = kh_ref[p, half], kl_ref[p, half]
            s_t = _dot_nt(kh, qh) + _dot_nt(kh, ql) + _dot_nt(kl, qh)
            tops.append(_topk_rows(s_t, rowid, PEER_TOPK))
        (s1, i1), (s2, i2) = tops
        cand = jnp.concatenate([s1[a:a + 1] + s2[:w] for a, w in enumerate(_PAIR_WIDTH)], axis=0)
        cidx = jnp.concatenate([i1[a:a + 1] * float(PEER_NKEYS) + i2[:w] for a, w in enumerate(_PAIR_WIDTH)],
                               axis=0)
        top, eid = _topk_rows(cand, pos, PEER_TOPK, payload=cidx)
        e = jnp.exp(top - top[0:1])
        rs = slice(p * PEER_TOPK, (p + 1) * PEER_TOPK)
        idx_ref[rs, :] = eid.astype(jnp.int32)
        gate_ref[rs, :] = e / e.sum(axis=0, keepdims=True)


def _peer_route(h2, p, tok_off=0, T=None):
    T = h2.shape[0] if T is None else T
    D = h2.shape[1]
    tm = min(T, 256)
    off = tok_off // tm
    wspec = pl.BlockSpec((D, 2 * PEER_HEADS * PEER_HALF), lambda i: (0, 0))
    kspec = pl.BlockSpec((PEER_HEADS, 2, PEER_NKEYS, PEER_HALF), lambda i: (0, 0, 0, 0))
    return pl.pallas_call(
        _route_kernel,
        grid=(T // tm,),
        in_specs=[pl.BlockSpec((tm, D), lambda i: (i + off, 0)), wspec, wspec,
                  pl.BlockSpec((1, PEER_HALF), lambda i: (0, 0)), kspec, kspec],
        out_specs=[pl.BlockSpec((PEER_PICKS, tm), lambda i: (0, i)),
                   pl.BlockSpec((PEER_PICKS, tm), lambda i: (0, i))],
        out_shape=[jax.ShapeDtypeStruct((PEER_PICKS, T), jnp.int32),
                   jax.ShapeDtypeStruct((PEER_PICKS, T), F32)],
        compiler_params=_cparams(("arbitrary",), 48),
        name="peer_route",
    )(h2, p["w_pq_hi"], p["w_pq_lo"], p["pq_g"], p["sub_keys_hi"], p["sub_keys_lo"])


def _gather_kernel(idx_ref, gate_ref, h_ref, x_ref, g2_ref, uv_hbm, dep_hbm, o_ref, buf, sem):
    tm = h_ref.shape[0]
    nbuf = buf.shape[0]
    ahead = nbuf - 1

    def issue(t, slot):
        for k in range(PEER_PICKS):
            row = pl.multiple_of(idx_ref[t * PEER_PICKS + k] * UV_CHUNKS, UV_CHUNKS)
            pltpu.make_async_copy(uv_hbm.at[pl.ds(row, UV_CHUNKS)],
                                  buf.at[slot, pl.ds(k * UV_CHUNKS, UV_CHUNKS)], sem.at[slot]).start()

    def wait(slot):
        pltpu.make_async_copy(uv_hbm.at[pl.ds(0, PEER_PICKS * UV_CHUNKS)], buf.at[slot], sem.at[slot]).wait()

    for t0 in range(ahead):
        issue(t0, t0)

    lane = lax.broadcasted_iota(jnp.int32, (PEER_PICKS, tm), 1)
    half = UV_CHUNKS // 2

    def token(t, slot, h_row):
        wait(slot)
        acc = None
        for c in range(half):
            u_c = buf[slot, pl.ds(c, PEER_PICKS, stride=UV_CHUNKS), :]
            term = u_c * h_row[:, c * LANES:(c + 1) * LANES]
            acc = term if acc is None else acc + term
        d = acc.sum(axis=1, keepdims=True)
        act = 0.5 * d * (1.0 + lax.erf(d * (2.0 ** -0.5)))
        gcol = jnp.where(lane == t, gate_ref[...], 0.0).sum(axis=1, keepdims=True)
        w = gcol * act
        mix = []
        for c in range(half):
            v_c = buf[slot, pl.ds(half + c, PEER_PICKS, stride=UV_CHUNKS), :]
            mix.append((v_c * w).sum(axis=0, keepdims=True))
        return jnp.concatenate(mix, axis=1)

    def group(g, carry):
        t0 = pl.multiple_of(g * nbuf, nbuf)
        h_rows = h_ref[pl.ds(t0, nbuf), :]
        outs = []
        for j in range(nbuf):
            nxt = t0 + j + ahead

            @pl.when(nxt < tm)
            def _():
                issue(nxt, (j + ahead) % nbuf)

            outs.append(token(t0 + j, j, h_rows[j:j + 1]))
        o_ref[pl.ds(t0, nbuf), :] = x_ref[pl.ds(t0, nbuf), :] + g2_ref[0] * jnp.concatenate(outs, axis=0)
        return carry

    lax.fori_loop(0, tm // nbuf, group, 0)


def _peer_gather(idx_flat, gate_t, h2, x, g2, uv, tokens_per_batch, tok_off, n_tok, dep=None, base=0):
    D = h2.shape[1]
    tm = GATHER_TM
    off = tok_off // tm
    goff = (base + tok_off) // tm
    return pl.pallas_call(
        _gather_kernel,
        grid=(n_tok // tm,),
        in_specs=[pl.BlockSpec((tm * PEER_PICKS,), lambda i: (i + off,), memory_space=pltpu.SMEM),
                  pl.BlockSpec((PEER_PICKS, tm), lambda i: (0, i + off)),
                  pl.BlockSpec((tm, D), lambda i: (i + goff, 0)),
                  pl.BlockSpec((tm, D), lambda i: (i + goff, 0)),
                  pl.BlockSpec((1, 1, D), lambda i: ((base + tok_off + i * tm) // tokens_per_batch, 0, 0)),
                  pl.BlockSpec(memory_space=pl.ANY),
                  pl.BlockSpec(memory_space=pl.ANY)],
        out_specs=pl.BlockSpec((tm, D), lambda i: (i, 0)),
        out_shape=jax.ShapeDtypeStruct((n_tok, D), F32),
        scratch_shapes=[pltpu.VMEM((GATHER_NBUF, PEER_PICKS * UV_CHUNKS, LANES), F32),
                        pltpu.SemaphoreType.DMA((GATHER_NBUF,))],
        compiler_params=_cparams(("arbitrary",), 32),
        name="peer_gather",
    )(idx_flat, gate_t, h2, x, g2, uv, g2 if dep is None else dep)


SC_WORKERS = 32
SC_LANES = 16
SC_STREAM_PICKS = 32
SC_STREAMS = PEER_PICKS // SC_STREAM_PICKS
SC_INDEX_TOKENS = 32
SC_COL_UNROLL = 4


def _sc_params():
    cp = pltpu.CompilerParams()
    if "needs_layout_passes" in pltpu.CompilerParams.__dataclass_fields__:
        cp = dataclasses.replace(cp, needs_layout_passes=False)
    return cp


def _sc_token_loop(table_hbm, idx_hbm, idx_v, bufs, per, load_token, compute, store_token):
    wid = lax.axis_index("s") * 2 + lax.axis_index("c")
    base = wid * per
    sb = min(per, SC_INDEX_TOKENS)

    def start(i, ch, b):
        rows, sem = bufs[b]
        pltpu.async_copy(table_hbm.at[idx_v.at[pl.ds(i * PEER_PICKS + ch * SC_STREAM_PICKS, SC_STREAM_PICKS)]],
                         rows, sem)

    def wait(b):
        rows, sem = bufs[b]
        pltpu.make_async_copy(table_hbm.at[idx_v.at[pl.ds(0, SC_STREAM_PICKS)]], rows, sem).wait()

    @pl.loop(0, per // sb)
    def _(blk):
        tok0 = base + blk * sb
        pltpu.sync_copy(idx_hbm.at[pl.ds(tok0 * PEER_PICKS, sb * PEER_PICKS)], idx_v)
        start(0, 0, 0)
        start(0, 1, 1)

        @pl.loop(0, sb)
        def _(i):
            nxt = jnp.minimum(i + 1, sb - 1)
            load_token(tok0 + i)
            for ch in range(SC_STREAMS):
                b = ch % 2
                wait(b)
                compute(ch, bufs[b][0])
                if ch + 2 < SC_STREAMS:
                    start(i, ch + 2, b)
                else:
                    start(nxt, ch + 2 - SC_STREAMS, b)
            store_token(tok0 + i)

        wait(0)
        wait(1)


def _sc_scratch(per):
    sb = min(per, SC_INDEX_TOKENS)
    rows = pltpu.VMEM((SC_STREAM_PICKS, D_MODEL), F32)
    return [pltpu.VMEM((sb * PEER_PICKS,), jnp.int32), rows, rows, pltpu.SemaphoreType.DMA, pltpu.SemaphoreType.DMA]


def _sc_dots(u_tab, idx_flat, h_flat, tok_off, n_tok):
    per = n_tok // SC_WORKERS
    L = SC_LANES

    @functools.partial(
        pl.kernel, mesh=plsc.VectorSubcoreMesh(core_axis_name="c", subcore_axis_name="s"),
        out_type=jax.ShapeDtypeStruct((n_tok * PEER_PICKS,), F32),
        scratch_types=_sc_scratch(per) + [pltpu.VMEM((D_MODEL,), F32), pltpu.VMEM((PEER_PICKS,), F32)],
        compiler_params=_sc_params(),
    )
    def kern(u_hbm, idx_hbm, x_hbm, d_hbm, idx_v, rows0, rows1, sem0, sem1, x_v, d_v):
        lane = lax.iota(jnp.int32, L)

        def load_token(t):
            pltpu.sync_copy(x_hbm.at[tok_off + t], x_v)

        def compute(ch, rows):
            for g in range(SC_STREAM_PICKS // L):
                def cstep(cb, accs):
                    accs = list(accs)
                    for cc in range(SC_COL_UNROLL):
                        col = pl.ds((cb * SC_COL_UNROLL + cc) * L, L)
                        xv = x_v[col]
                        for j in range(L):
                            accs[j] = accs[j] + rows[g * L + j, col] * xv
                    return tuple(accs)

                accs = lax.fori_loop(0, D_MODEL // (L * SC_COL_UNROLL), cstep,
                                     tuple(jnp.zeros((L,), F32) for _ in range(L)))
                dvec = jnp.zeros((L,), F32)
                for j in range(L):
                    dvec = jnp.where(lane == j, jnp.sum(accs[j]), dvec)
                d_v[pl.ds(ch * SC_STREAM_PICKS + g * L, L)] = dvec

        def store_token(t):
            pltpu.sync_copy(d_v, d_hbm.at[pl.ds(t * PEER_PICKS, PEER_PICKS)])

        _sc_token_loop(u_hbm, idx_hbm, idx_v, ((rows0, sem0), (rows1, sem1)), per, load_token, compute,
                       store_token)

    return kern(u_tab, idx_flat, h_flat)


def _sc_mix(v_tab, idx_flat, w_flat, n_tok):
    per = n_tok // SC_WORKERS
    L = SC_LANES

    @functools.partial(
        pl.kernel, mesh=plsc.VectorSubcoreMesh(core_axis_name="c", subcore_axis_name="s"),
        out_type=jax.ShapeDtypeStruct((n_tok, D_MODEL), F32),
        scratch_types=_sc_scratch(per) + [pltpu.VMEM((PEER_PICKS,), F32), pltpu.VMEM((D_MODEL,), F32)],
        compiler_params=_sc_params(),
    )
    def kern(v_hbm, idx_hbm, w_hbm, o_hbm, idx_v, rows0, rows1, sem0, sem1, w_v, o_v):
        def load_token(t):
            pltpu.sync_copy(w_hbm.at[pl.ds(t * PEER_PICKS, PEER_PICKS)], w_v)

        def compute(ch, rows):
            for cg in range(D_MODEL // (L * L)):
                cols = [pl.ds((cg * L + c) * L, L) for c in range(L)]
                init = (tuple(jnp.zeros((L,), F32) for _ in cols) if ch == 0 else tuple(o_v[c] for c in cols))

                def kstep(k, accs):
                    wk = plsc.load_gather(w_v, [jnp.full((L,), ch * SC_STREAM_PICKS, jnp.int32) + k])
                    return tuple(acc + rows[k, c] * wk for acc, c in zip(accs, cols))

                accs = lax.fori_loop(0, SC_STREAM_PICKS, kstep, init)
                for acc, c in zip(accs, cols):
                    o_v[c] = acc

        def store_token(t):
            pltpu.sync_copy(o_v, o_hbm.at[t])

        _sc_token_loop(v_hbm, idx_hbm, idx_v, ((rows0, sem0), (rows1, sem1)), per, load_token, compute,
                       store_token)

    return kern(v_tab, idx_flat, w_flat)


def _act_kernel(d_ref, g_ref, dep_ref, w_ref):
    d = d_ref[...]
    w_ref[...] = g_ref[...] * (0.5 * d * (1.0 + lax.erf(d * (2.0 ** -0.5))))


def _peer_act(d, gate, dep):
    n = d.shape[0]
    tm = min(n, 1024)
    blk = pl.BlockSpec((tm, PEER_PICKS), lambda i: (i, 0))
    return pl.pallas_call(
        _act_kernel, grid=(n // tm,),
        in_specs=[blk, blk, pl.BlockSpec(memory_space=pl.ANY)], out_specs=blk,
        out_shape=jax.ShapeDtypeStruct(d.shape, F32),
        compiler_params=_cparams(("arbitrary",), 32), name="peer_act",
    )(d, gate, dep)


def _residual_kernel(x_ref, g2_ref, m_ref, o_ref):
    o_ref[...] = x_ref[...] + g2_ref[0] * m_ref[...]


def _peer_residual(x, g2, mix, tok_off, tokens_per_batch):
    n, D = mix.shape
    tm = min(n, 512)
    off = tok_off // tm
    return pl.pallas_call(
        _residual_kernel, grid=(n // tm,),
        in_specs=[pl.BlockSpec((tm, D), lambda i: (i + off, 0)),
                  pl.BlockSpec((1, 1, D), lambda i: ((tok_off + i * tm) // tokens_per_batch, 0, 0)),
                  pl.BlockSpec((tm, D), lambda i: (i, 0))],
        out_specs=pl.BlockSpec((tm, D), lambda i: (i, 0)),
        out_shape=jax.ShapeDtypeStruct((n, D), F32),
        compiler_params=_cparams(("arbitrary",), 32), name="peer_residual",
    )(x, g2, mix)


PEER_WAVES = 2
WAVE_SC_16THS = 9


def _sc_share(n_tok):
    nw = n_tok // PEER_WAVES
    sc = nw * WAVE_SC_16THS // 16
    tc = nw - sc
    ok = (n_tok % (16 * PEER_WAVES) == 0 and sc % (SC_WORKERS * SC_INDEX_TOKENS) == 0 and sc % 1024 == 0
          and tc % (2 * GATHER_TM) == 0 and tc % 512 == 0 and nw % 512 == 0)
    return sc * PEER_WAVES if ok else 0


def _peer(h2, x, g2, p, sc_tokens=0):
    B, T, D = x.shape
    n = B * T
    hf = h2.reshape(n, D)
    xf = x.reshape(n, D)
    if sc_tokens == 0:
        idx_t, gate_t = _peer_route(hf, p)
        out = _peer_gather(idx_t.T.reshape(-1), gate_t, hf, xf, g2, p["uv"], T, 0, n)
        return out.reshape(B, T, D)
    nw = n // PEER_WAVES
    sc = nw * WAVE_SC_16THS // 16
    n_tc = nw - sc
    half = n_tc // 2
    outs = []
    dep = None
    for w in range(PEER_WAVES):
        w0 = w * nw
        idx_t, gate_t = _peer_route(hf, p, w0, nw)
        idx_sc = idx_t[:, n_tc:].T.reshape(-1)
        gate_sc = gate_t[:, n_tc:].T
        idx_tc = idx_t[:, :n_tc].T.reshape(-1)
        d = _sc_dots(p["peer_u"], idx_sc, hf, w0 + n_tc, sc)
        out_a = _peer_gather(idx_tc, gate_t, hf, xf, g2, p["uv"], T, 0, half, dep=dep, base=w0)
        wgt = _peer_act(d.reshape(sc, PEER_PICKS), gate_sc, out_a)
        mix = _sc_mix(p["peer_v"], idx_sc, wgt.reshape(-1), sc)
        out_b = _peer_gather(idx_tc, gate_t, hf, xf, g2, p["uv"], T, half, n_tc - half, dep=wgt, base=w0)
        outs += [out_a, out_b, _peer_residual(xf, g2, mix, w0 + n_tc, T)]
        dep = out_b
    return jnp.concatenate(outs, axis=0).reshape(B, T, D)


def _rearrange_w_in(w):
    cols = [w[:, o:o + n] for _, o, n, _ in _SEGS]
    cols.append(jnp.zeros((w.shape[0], PROJ_W - sum(n for _, _, n, _ in _SEGS)), w.dtype))
    return jnp.concatenate(cols, axis=1).astype(BF16)


def _rope_tables(S):
    t = jnp.arange(S)
    row = (t // GRID_W).astype(F32)[:, None]
    colp = (t % GRID_W).astype(F32)[:, None]
    nf = HEAD_DIM // 4
    inv = ROPE_THETA ** (-jnp.arange(nf, dtype=F32) / nf)
    ang = jnp.concatenate([row * inv, row * inv, colp * inv, colp * inv], axis=-1)
    ang = jnp.concatenate([ang, ang], axis=-1)
    return jnp.cos(ang), jnp.sin(ang)


def _layer_params(i, a):
    two = lambda g: jnp.concatenate([g, g]).reshape(1, LANES)
    hid = np.arange(LANES) // HEAD_DIM
    E = PEER_EXPERTS
    w_pq_hi, w_pq_lo = _split_bf16(a["w_pq"][i])
    keys_hi, keys_lo = _split_bf16(a["sub_keys"][i])
    return {
        "w_mod": a["w_mod"][i], "b_mod": a["b_mod"][i],
        "norm1": a["norm1"][i], "norm2": a["norm2"][i].reshape(1, D_MODEL),
        "w_in": _rearrange_w_in(a["w_in"][i]),
        "qn_a2": two(a["qn_a"][i]), "kn_a2": two(a["kn_a"][i]),
        "qn_b2": two(a["qn_b"][i]), "kn_b2": two(a["kn_b"][i]),
        "m128": jnp.asarray(hid[:, None] == hid[None, :], BF16),
        "sink_a": a["sink_a"][i], "rpb_b": a["rpb_b"][i],
        "wa2_f": a["wa2_f"][i], "wa2_b": a["wa2_b"][i],
        "ba_f": a["ba_f"][i].reshape(1, -1), "ba_b": a["ba_b"][i].reshape(1, -1),
        "gn_c": a["gn_c"][i].reshape(1, C_DV),
        "w_oa": a["w_oa"][i].astype(BF16), "w_ob": a["w_ob"][i].astype(BF16),
        "w_oc": a["w_oc"][i].astype(BF16), "w_out": a["w_out"][i].astype(BF16),
        "w_pq_hi": w_pq_hi, "w_pq_lo": w_pq_lo, "pq_g": a["pq_g"][i].reshape(1, PEER_HALF),
        "sub_keys_hi": keys_hi, "sub_keys_lo": keys_lo,
        "peer_u": a["peer_u"][i], "peer_v": a["peer_v"][i],
        "uv": jnp.concatenate([a["peer_u"][i].reshape(E, UV_CHUNKS // 2, LANES),
                               a["peer_v"][i].reshape(E, UV_CHUNKS // 2, LANES)],
                              axis=1).reshape(E * UV_CHUNKS, LANES),
    }


def _layer(x, ctx, c_all, p, cos, sin, last):
    B, S, D = x.shape
    L = ctx.shape[1]
    ones = jnp.ones((L, LANES), F32)
    zeros = jnp.zeros((L, LANES), F32)
    state0 = jnp.zeros((B, C_HEADS, C_DV, C_DK), F32)
    mods = _modulation(c_all, p["w_mod"], p["b_mod"])
    lat = [mods[:B, k * D:(k + 1) * D].reshape(B, 1, D) for k in range(N_MOD)]
    cm = [jnp.broadcast_to(mods[B, k * D:(k + 1) * D].reshape(1, 1, D), (B, 1, D)) for k in range(N_MOD)]
    sh1, sc1, g1, sh2, sc2, g2 = lat
    csh1, csc1, cg1, csh2, csc2, cg2 = cm

    proj = _norm_proj(x, p["norm1"], sh1, sc1, p["w_in"])
    cproj = _norm_proj(ctx, p["norm1"], csh1, csc1, p["w_in"])
    qa, ka, qb, kb = _qk_prep(proj, cos, sin, p)
    cqa, cka, cqb, ckb = _qk_prep(cproj, ones, zeros, p)

    o_a = _attn_a(qa, ka, proj, cka, cproj, p["sink_a"])
    o_b = _attn_b(qb, kb, proj, ckb, cproj, _na_bias(p["rpb_b"], S // GRID_W))
    co_f, co_b, s_f, s_b = _gla(cproj, p, state0, state0)
    o_f, o_bk, _, _ = _gla(proj, p, s_f, s_b)
    x, h2 = _merge(o_a, o_b, o_f, o_bk, proj, x, g1, sh2, sc2, p)
    x = _peer(h2, x, g2, p, sc_tokens=_sc_share(B * S))
    if not last:
        co_a = _ctx_attn(cqa, cka, cproj, "vA", p["sink_a"], A_KV_HEADS, True)
        co_bb = _ctx_attn(cqb, ckb, cproj, "vB", p["sink_a"], B_HEADS, False)
        ctx, hc2 = _merge(co_a, co_bb, co_f, co_b, cproj, ctx, cg1, csh2, csc2, p)
        ctx = _peer(hc2, ctx, cg2, p)
    return x, ctx


def _forward(a):
    x, ctx = a["x"], a["ctx"]
    B, S, D = x.shape
    cos, sin = _rope_tables(S)
    c_all = jnp.zeros((8, D), F32).at[:B].set(a["c"]).at[B].set(a["c_ctx"])
    for i in range(DEPTH):
        x, ctx = _layer(x, ctx, c_all, _layer_params(i, a), cos, sin, i == DEPTH - 1)
    return x


def kernel(x, c, ctx, c_ctx, w_mod, b_mod, norm1, norm2, w_in, qn_a, kn_a, sink_a, qn_b, kn_b, rpb_b,
           wa2_f, ba_f, wa2_b, ba_b, gn_c, w_oa, w_ob, w_oc, w_out, w_pq, pq_g, sub_keys, peer_u, peer_v):
    return _forward(dict(
        x=x, c=c, ctx=ctx, c_ctx=c_ctx, w_mod=w_mod, b_mod=b_mod, norm1=norm1, norm2=norm2, w_in=w_in,
        qn_a=qn_a, kn_a=kn_a, sink_a=sink_a, qn_b=qn_b, kn_b=kn_b, rpb_b=rpb_b, wa2_f=wa2_f, ba_f=ba_f,
        wa2_b=wa2_b, ba_b=ba_b, gn_c=gn_c, w_oa=w_oa, w_ob=w_ob, w_oc=w_oc, w_out=w_out, w_pq=w_pq,
        pq_g=pq_g, sub_keys=sub_keys, peer_u=peer_u, peer_v=peer_v))
```
